```python
import jax, jax.numpy as jnp
from jax import lax
import numpy as np

D_MODEL = 1024
BATCH = 32
SEQ = 256
DEPTH = 4
DEC_BATCH = 4
DEC_SEQ = 4096
PAST_LEN = 256

GRID_W = 64
N_MOD = 9
D_FF = 2816
N_CM_LAYERS = (DEPTH + 1) // 2
N_ATTN_LAYERS = DEPTH // 2
CONV_CH = D_MODEL // 2
CONV_WIDTH = 31
SGU_CH = D_MODEL // 2
SGU_GROUPS = 4
CHUNK = 128
SGU_GC = SGU_CH // SGU_GROUPS
HEAD_DIM = 64
N_HEADS = D_MODEL // HEAD_DIM
N_KV_HEADS = 4
GROUP = N_HEADS // N_KV_HEADS
Q_DIM = N_HEADS * HEAD_DIM
KV_DIM = N_KV_HEADS * HEAD_DIM
BLOCK = 128
WINDOW = 128
ROPE_HALF = HEAD_DIM // 2
ROPE_BASE = 10000.0
EPS = 1e-6
NEG_INF = -1e30
ATTN_SCALE = HEAD_DIM ** -0.5

kernel_name = "hybrid_dit_conv_gmlp_swa_step"


def rms_norm(x, g):
    xf = x.astype(jnp.float32)
    y = xf * lax.rsqrt(jnp.mean(xf * xf, axis=-1, keepdims=True) + EPS)
    return (y * g.astype(jnp.float32)).astype(x.dtype)


def layer_norm(x, g, b):
    xf = x.astype(jnp.float32)
    mu = jnp.mean(xf, axis=-1, keepdims=True)
    var = jnp.mean(jnp.square(xf - mu), axis=-1, keepdims=True)
    y = (xf - mu) * lax.rsqrt(var + EPS)
    return (y * g.astype(jnp.float32) + b.astype(jnp.float32)).astype(x.dtype)


def modulation(cond, w_mod, b_mod):
    m = jax.nn.silu(cond) @ w_mod + b_mod
    return m.reshape(cond.shape[0], N_MOD, D_MODEL)


def pre(x, m, g, slot):
    h = rms_norm(x, g)
    return h * (1 + m[:, 3 * slot + 1][:, None, :]) + m[:, 3 * slot][:, None, :]


def residual(x, y, m, g, slot, weight):
    return x + weight * m[:, 3 * slot + 2][:, None, :] * rms_norm(y, g)


def swiglu(h, wg, wu, wd):
    return (jax.nn.silu(h @ wg) * (h @ wu)) @ wd


def conv_gmlp(h, w_in, conv_w, conv_b, cln_g, cln_b, sln_g, sln_b, sgu_w, sgu_b, w_out):
    B, T, _ = h.shape
    z = h @ w_in
    a_val, a_gate, u, v = jnp.split(z, [CONV_CH, 2 * CONV_CH, 2 * CONV_CH + SGU_CH], axis=-1)
    g = a_val * jax.nn.sigmoid(a_gate)
    g = lax.conv_general_dilated(
        g, conv_w[:, None, :], window_strides=(1,),
        padding=[(CONV_WIDTH // 2, CONV_WIDTH // 2)],
        dimension_numbers=('NWC', 'WIO', 'NWC'),
        feature_group_count=CONV_CH) + conv_b
    a_out = jax.nn.silu(layer_norm(g, cln_g, cln_b))
    u = jax.nn.gelu(u)
    v = layer_norm(jax.nn.gelu(v), sln_g, sln_b)
    vc = v.reshape(B, T // CHUNK, CHUNK, SGU_GROUPS, SGU_GC)
    vc = jnp.einsum('gpq,bnqgc->bnpgc', sgu_w, vc) + sgu_b.T[:, :, None]
    b_out = u * vc.reshape(B, T, SGU_CH)
    return jnp.concatenate([a_out, b_out], axis=-1) @ w_out


def split_qkv(z):
    B, T, _ = z.shape
    q = z[..., :Q_DIM].reshape(B, T, N_KV_HEADS, GROUP, HEAD_DIM)
    k = z[..., Q_DIM:Q_DIM + KV_DIM].reshape(B, T, N_KV_HEADS, HEAD_DIM)
    v = z[..., Q_DIM + KV_DIM:].reshape(B, T, N_KV_HEADS, HEAD_DIM)
    return q, k, v


def axial_angles(T):
    rows = T // GRID_W
    r = jnp.repeat(jnp.arange(rows), GRID_W).astype(jnp.float32)
    col = jnp.tile(jnp.arange(GRID_W), rows).astype(jnp.float32)
    inv_freq = jnp.power(ROPE_BASE, -jnp.arange(0, ROPE_HALF, 2, dtype=jnp.float32) / ROPE_HALF)
    return r[:, None] * inv_freq[None, :], col[:, None] * inv_freq[None, :]


def rope_half(xp, ang):
    T, F = ang.shape
    shp = (1, T) + (1,) * (xp.ndim - 3) + (F,)
    cos = jnp.cos(ang).reshape(shp).astype(xp.dtype)
    sin = jnp.sin(ang).reshape(shp).astype(xp.dtype)
    a, b = xp[..., :F], xp[..., F:]
    return jnp.concatenate([a * cos - b * sin, b * cos + a * sin], axis=-1)


def axial_rope(x, ang_r, ang_c):
    return jnp.concatenate([rope_half(x[..., :ROPE_HALF], ang_r),
                            rope_half(x[..., ROPE_HALF:], ang_c)], axis=-1)


def sink_softmax(s, sink):
    col = jnp.broadcast_to(sink.astype(jnp.float32).reshape(1, N_KV_HEADS, GROUP, 1, 1),
                           s.shape[:-1] + (1,))
    p = jax.nn.softmax(jnp.concatenate([s, col], axis=-1), axis=-1)
    return p[..., :-1]


def context_attention(q, k, v, sink):
    B, L = q.shape[:2]

    def one_block(i):
        qb = lax.dynamic_slice_in_dim(q, i * BLOCK, BLOCK, axis=1)
        s = jnp.einsum('bqkgd,bskd->bkgqs', qb, k).astype(jnp.float32) * ATTN_SCALE
        p = sink_softmax(s, sink).astype(v.dtype)
        return jnp.einsum('bkgqs,bskd->bqkgd', p, v)

    o = lax.map(one_block, jnp.arange(L // BLOCK))
    return jnp.moveaxis(o, 0, 1).reshape(B, L, Q_DIM)


def latent_attention(q, k, v, k_ctx, v_ctx, sink):
    B, T = q.shape[:2]
    pad = ((0, 0), (BLOCK, BLOCK), (0, 0), (0, 0))
    kpad = jnp.pad(k, pad)
    vpad = jnp.pad(v, pad)

    def one_block(i):
        start = i * BLOCK
        qb = lax.dynamic_slice_in_dim(q, start, BLOCK, axis=1)
        kb = lax.dynamic_slice_in_dim(kpad, start, 3 * BLOCK, axis=1)
        vb = lax.dynamic_slice_in_dim(vpad, start, 3 * BLOCK, axis=1)
        qpos = start + jnp.arange(BLOCK)
        kpos = start - BLOCK + jnp.arange(3 * BLOCK)
        valid = ((jnp.abs(qpos[:, None] - kpos[None, :]) <= WINDOW)
                 & (kpos[None, :] >= 0) & (kpos[None, :] < T))
        s_loc = jnp.einsum('bqkgd,bskd->bkgqs', qb, kb).astype(jnp.float32) * ATTN_SCALE
        s_loc = jnp.where(valid, s_loc, NEG_INF)
        s_ctx = jnp.einsum('bqkgd,bskd->bkgqs', qb, k_ctx).astype(jnp.float32) * ATTN_SCALE
        p = sink_softmax(jnp.concatenate([s_loc, s_ctx], axis=-1), sink).astype(v.dtype)
        return (jnp.einsum('bkgqs,bskd->bqkgd', p[..., :3 * BLOCK], vb)
                + jnp.einsum('bkgqs,bskd->bqkgd', p[..., 3 * BLOCK:], v_ctx))

    o = lax.map(one_block, jnp.arange(T // BLOCK))
    return jnp.moveaxis(o, 0, 1).reshape(B, T, Q_DIM)


def setup_inputs(seed: int = 0) -> dict:
    key = jax.random.key(seed)
    ks = iter(jax.random.split(key, 32))
    f32 = jnp.float32

    def nrm(shape, scale=1.0):
        return jax.random.normal(next(ks), shape, f32) * scale

    return {
        "x_prompt": nrm((BATCH, SEQ, D_MODEL)),
        "x_sample": nrm((DEC_BATCH, DEC_SEQ, D_MODEL)),
        "cache_k": nrm((DEC_BATCH, N_ATTN_LAYERS, PAST_LEN, N_KV_HEADS, HEAD_DIM)),
        "cache_v": nrm((DEC_BATCH, N_ATTN_LAYERS, PAST_LEN, N_KV_HEADS, HEAD_DIM)),
        "c": nrm((DEC_BATCH, D_MODEL)),
        "c_ctx": nrm((D_MODEL,)),
        "w_mod": nrm((DEPTH, D_MODEL, N_MOD * D_MODEL), D_MODEL ** -0.5),
        "b_mod": nrm((DEPTH, N_MOD * D_MODEL), 0.02),
        "norm_w": 1.0 + nrm((DEPTH, 6, D_MODEL), 0.02),
        "ffn_w_gate": nrm((DEPTH, 2, D_MODEL, D_FF), D_MODEL ** -0.5),
        "ffn_w_up": nrm((DEPTH, 2, D_MODEL, D_FF), D_MODEL ** -0.5),
        "ffn_w_down": nrm((DEPTH, 2, D_FF, D_MODEL), D_FF ** -0.5),
        "cm_w_in": nrm((N_CM_LAYERS, D_MODEL, 2 * CONV_CH + 2 * SGU_CH), D_MODEL ** -0.5),
        "cm_conv_w": nrm((N_CM_LAYERS, CONV_WIDTH, CONV_CH), CONV_WIDTH ** -0.5),
        "cm_conv_b": nrm((N_CM_LAYERS, CONV_CH), 0.02),
        "cm_conv_ln_g": 1.0 + nrm((N_CM_LAYERS, CONV_CH), 0.02),
        "cm_conv_ln_b": nrm((N_CM_LAYERS, CONV_CH), 0.02),
        "cm_sgu_ln_g": 1.0 + nrm((N_CM_LAYERS, SGU_CH), 0.02),
        "cm_sgu_ln_b": nrm((N_CM_LAYERS, SGU_CH), 0.02),
        "cm_sgu_w": nrm((N_CM_LAYERS, SGU_GROUPS, CHUNK, CHUNK), CHUNK ** -0.5),
        "cm_sgu_b": nrm((N_CM_LAYERS, SGU_GROUPS, CHUNK), 0.02),
        "cm_w_out": nrm((N_CM_LAYERS, CONV_CH + SGU_CH, D_MODEL), (CONV_CH + SGU_CH) ** -0.5),
        "attn_w_qkv": nrm((N_ATTN_LAYERS, D_MODEL, Q_DIM + 2 * KV_DIM), D_MODEL ** -0.5),
        "attn_w_o": nrm((N_ATTN_LAYERS, Q_DIM, D_MODEL), Q_DIM ** -0.5),
        "attn_sink": nrm((N_ATTN_LAYERS, N_HEADS), 0.5),
    }


def reference(x_prompt, x_sample, cache_k, cache_v, c, c_ctx, w_mod, b_mod, norm_w,
              ffn_w_gate, ffn_w_up, ffn_w_down, cm_w_in, cm_conv_w, cm_conv_b,
              cm_conv_ln_g, cm_conv_ln_b, cm_sgu_ln_g, cm_sgu_ln_b, cm_sgu_w, cm_sgu_b,
              cm_w_out, attn_w_qkv, attn_w_o, attn_sink):
    xp, xs = x_prompt, x_sample
    ang_r, ang_c = axial_angles(xs.shape[1])
    new_k, new_v = [], []
    for l in range(DEPTH):
        mp = modulation(c_ctx[None, :], w_mod[l], b_mod[l])
        ms = modulation(c, w_mod[l], b_mod[l])
        f1 = (ffn_w_gate[l, 0], ffn_w_up[l, 0], ffn_w_down[l, 0])
        xp = residual(xp, swiglu(pre(xp, mp, norm_w[l, 0], 0), *f1), mp, norm_w[l, 1], 0, 0.5)
        xs = residual(xs, swiglu(pre(xs, ms, norm_w[l, 0], 0), *f1), ms, norm_w[l, 1], 0, 0.5)
        hp = pre(xp, mp, norm_w[l, 2], 1)
        hs = pre(xs, ms, norm_w[l, 2], 1)
        j = l // 2
        if l % 2 == 0:
            cm = (cm_w_in[j], cm_conv_w[j], cm_conv_b[j], cm_conv_ln_g[j], cm_conv_ln_b[j],
                  cm_sgu_ln_g[j], cm_sgu_ln_b[j], cm_sgu_w[j], cm_sgu_b[j], cm_w_out[j])
            yp = conv_gmlp(hp, *cm)
            ys = conv_gmlp(hs, *cm)
        else:
            qp, kp, vp = split_qkv(hp @ attn_w_qkv[j])
            yp = context_attention(qp, kp, vp, attn_sink[j].reshape(N_KV_HEADS, GROUP)) @ attn_w_o[j]
            new_k.append(kp)
            new_v.append(vp)
            qs, ks_, vs = split_qkv(hs @ attn_w_qkv[j])
            qs = axial_rope(qs, ang_r, ang_c)
            ks_ = axial_rope(ks_, ang_r, ang_c)
            ys = latent_attention(qs, ks_, vs, cache_k[:, j], cache_v[:, j],
                                  attn_sink[j].reshape(N_KV_HEADS, GROUP)) @ attn_w_o[j]
        xp = residual(xp, yp, mp, norm_w[l, 3], 1, 1.0)
        xs = residual(xs, ys, ms, norm_w[l, 3], 1, 1.0)
        f2 = (ffn_w_gate[l, 1], ffn_w_up[l, 1], ffn_w_down[l, 1])
        xp = residual(xp, swiglu(pre(xp, mp, norm_w[l, 4], 2), *f2), mp, norm_w[l, 5], 2, 0.5)
        xs = residual(xs, swiglu(pre(xs, ms, norm_w[l, 4], 2), *f2), ms, norm_w[l, 5], 2, 0.5)
    return (xp, xs, jnp.stack(new_k, axis=1), jnp.stack(new_v, axis=1))
```

```python
import functools

import numpy as np
import jax
import jax.numpy as jnp
from jax import lax
from jax.experimental import pallas as pl
from jax.experimental.pallas import tpu as pltpu

F32 = jnp.float32
BF16 = jnp.bfloat16

D_MODEL = 1024
DEPTH = 4
GRID_W = 64
N_MOD = 9
D_FF = 2816
CONV_CH = 512
CONV_WIDTH = 31
CONV_HALF = CONV_WIDTH // 2
SGU_CH = 512
SGU_GROUPS = 4
CHUNK = 128
HEAD_DIM = 64
N_HEADS = 16
N_KV_HEADS = 4
GROUP = N_HEADS // N_KV_HEADS
Q_DIM = N_HEADS * HEAD_DIM
KV_DIM = N_KV_HEADS * HEAD_DIM
BLOCK = 128
ROPE_HALF = HEAD_DIM // 2
ROPE_BASE = 10000.0
EPS = 1e-6
NEG_INF = -1e30
ATTN_SCALE = HEAD_DIM ** -0.5

MOD_ROWS = 8
HALO = 16
MIB = 1024 * 1024


def _rms(x, g):
    return x * lax.rsqrt(jnp.mean(x * x, axis=-1, keepdims=True) + EPS) * g


def _ln(x, g, b):
    mu = jnp.mean(x, axis=-1, keepdims=True)
    xc = x - mu
    var = jnp.mean(xc * xc, axis=-1, keepdims=True)
    return xc * lax.rsqrt(var + EPS) * g + b


def _resident(block_shape, index):
    return pl.BlockSpec(block_shape, lambda i: index, pipeline_mode=pl.Buffered(1))


def _params(vmem_mib):
    return pltpu.CompilerParams(dimension_semantics=("arbitrary",),
                                vmem_limit_bytes=vmem_mib * MIB)


class _Layout:
    def __init__(self, n_prompt, prompt_len, n_dec, dec_len):
        self.n_prompt, self.prompt_len = n_prompt, prompt_len
        self.n_dec, self.dec_len = n_dec, dec_len
        self.tp = n_prompt * prompt_len
        self.ts = n_dec * dec_len
        self.t = self.tp + self.ts

    def mod_row(self, i, tm):
        p_tiles = self.tp // tm
        per_seq = self.dec_len // tm
        return jnp.where(i < p_tiles, 0, 1 + (i - p_tiles) // per_seq)


def _vec_spec(row):
    return pl.BlockSpec((None, 1, D_MODEL), lambda i: (row, 0, 0))


def _mod_spec(lay, tm, col):
    return pl.BlockSpec((None, 1, D_MODEL), lambda i: (lay.mod_row(i, tm), 0, col))


def _mod_kernel(c_ref, w_ref, b_ref, o_ref):
    c = c_ref[...]
    s = (c * jax.nn.sigmoid(c)).astype(BF16)
    o_ref[...] = jnp.dot(s, w_ref[...].astype(BF16), preferred_element_type=F32) + b_ref[...]


def _modulation(cond, w_mod, b_mod):
    tn = 1024
    n = N_MOD * D_MODEL
    return pl.pallas_call(
        _mod_kernel,
        grid=(DEPTH, n // tn),
        in_specs=[pl.BlockSpec((MOD_ROWS, D_MODEL), lambda l, j: (0, 0)),
                  pl.BlockSpec((None, D_MODEL, tn), lambda l, j: (l, 0, j)),
                  pl.BlockSpec((None, 1, tn), lambda l, j: (l, 0, j))],
        out_specs=pl.BlockSpec((None, MOD_ROWS, tn), lambda l, j: (l, 0, j)),
        out_shape=jax.ShapeDtypeStruct((DEPTH, MOD_ROWS, n), F32),
        compiler_params=pltpu.CompilerParams(dimension_semantics=("arbitrary", "arbitrary"),
                                             vmem_limit_bytes=24 * MIB),
        name="modulation",
    )(cond, w_mod, b_mod.reshape(DEPTH, 1, n))


def _ffn_kernel(x_ref, sh_ref, sc_ref, gt_ref, g1_ref, g2_ref, wg_ref, wu_ref, wd_ref, o_ref, *, weight):
    x = x_ref[...]
    h = (_rms(x, g1_ref[...]) * (1.0 + sc_ref[...]) + sh_ref[...]).astype(BF16)
    a = jnp.dot(h, wg_ref[...], preferred_element_type=F32)
    b = jnp.dot(h, wu_ref[...], preferred_element_type=F32)
    act = (a * jax.nn.sigmoid(a) * b).astype(BF16)
    y = jnp.dot(act, wd_ref[...], preferred_element_type=F32)
    o_ref[...] = x + weight * gt_ref[...] * _rms(y, g2_ref[...])


def _ffn(lay, x, mod_l, norm_w, wg, wu, wd, l, s, slot):
    tm = 512
    tile = pl.BlockSpec((tm, D_MODEL), lambda i: (i, 0))
    return pl.pallas_call(
        functools.partial(_ffn_kernel, weight=0.5),
        grid=(lay.t // tm,),
        in_specs=[tile,
                  _mod_spec(lay, tm, 3 * slot), _mod_spec(lay, tm, 3 * slot + 1),
                  _mod_spec(lay, tm, 3 * slot + 2),
                  _vec_spec(l * 6 + 2 * slot), _vec_spec(l * 6 + 2 * slot + 1),
                  _resident((None, None, D_MODEL, D_FF), (l, s, 0, 0)),
                  _resident((None, None, D_MODEL, D_FF), (l, s, 0, 0)),
                  _resident((None, None, D_FF, D_MODEL), (l, s, 0, 0))],
        out_specs=tile,
        out_shape=jax.ShapeDtypeStruct((lay.t, D_MODEL), F32),
        compiler_params=_params(52),
        name="swiglu",
    )(x, mod_l, mod_l, mod_l, norm_w, norm_w, wg, wu, wd)


def _cm_in_kernel(x_ref, sh_ref, sc_ref, g_ref, win_ref, slg_ref, slb_ref, sw_ref, sb_ref,
                  gout_ref, bout_ref, *, tm):
    x = x_ref[...]
    h = (_rms(x, g_ref[...]) * (1.0 + sc_ref[...]) + sh_ref[...]).astype(BF16)
    z = jnp.dot(h, win_ref[...], preferred_element_type=F32)
    a_val = z[:, :CONV_CH]
    a_gate = z[:, CONV_CH:2 * CONV_CH]
    u = jax.nn.gelu(z[:, 2 * CONV_CH:2 * CONV_CH + SGU_CH])
    v = _ln(jax.nn.gelu(z[:, 2 * CONV_CH + SGU_CH:]), slg_ref[...], slb_ref[...]).astype(BF16)
    gout_ref[...] = a_val * jax.nn.sigmoid(a_gate)
    gc = SGU_CH // SGU_GROUPS
    for n in range(tm // CHUNK):
        rows = slice(n * CHUNK, (n + 1) * CHUNK)
        for gi in range(SGU_GROUPS):
            cols = slice(gi * gc, (gi + 1) * gc)
            vc = jnp.dot(sw_ref[gi], v[rows, cols], preferred_element_type=F32) + sb_ref[:, gi:gi + 1]
            bout_ref[rows, cols] = u[rows, cols] * vc


def _cm_in(lay, x, mod_l, norm_w, w_in, sln_g, sln_b, sgu_w, sgu_bt, l, j):
    tm = 256
    tile = pl.BlockSpec((tm, D_MODEL), lambda i: (i, 0))
    half = pl.BlockSpec((tm, CONV_CH), lambda i: (i, 0))
    n_in = 2 * CONV_CH + 2 * SGU_CH
    return pl.pallas_call(
        functools.partial(_cm_in_kernel, tm=tm),
        grid=(lay.t // tm,),
        in_specs=[tile, _mod_spec(lay, tm, 3), _mod_spec(lay, tm, 4), _vec_spec(l * 6 + 2),
                  _resident((None, D_MODEL, n_in), (j, 0, 0)),
                  pl.BlockSpec((None, 1, SGU_CH), lambda i: (j, 0, 0)),
                  pl.BlockSpec((None, 1, SGU_CH), lambda i: (j, 0, 0)),
                  _resident((None, SGU_GROUPS, CHUNK, CHUNK), (j, 0, 0, 0)),
                  pl.BlockSpec((None, CHUNK, SGU_GROUPS), lambda i: (j, 0, 0))],
        out_specs=[half, half],
        out_shape=[jax.ShapeDtypeStruct((lay.t, CONV_CH), F32),
                   jax.ShapeDtypeStruct((lay.t, SGU_CH), F32)],
        compiler_params=_params(40),
        name="conv_gmlp_in",
    )(x, mod_l, mod_l, norm_w, w_in, sln_g, sln_b, sgu_w, sgu_bt)


def _cm_out_kernel(x_ref, gt_ref, g3_ref, gp_ref, gc_ref, gn_ref, bo_ref, cw_ref, cb_ref,
                   clg_ref, clb_ref, wout_ref, o_ref, gbuf, abuf, *, tm, p_tiles, per_seq):
    i = pl.program_id(0)
    local = (i - p_tiles) % per_seq
    has_left = jnp.logical_and(i >= p_tiles, local != 0)
    has_right = jnp.logical_and(i >= p_tiles, local != per_seq - 1)
    gbuf[0:HALO, :] = jnp.where(has_left, gp_ref[...], 0.0)
    gbuf[HALO:HALO + tm, :] = gc_ref[...]
    gbuf[HALO + tm:, :] = jnp.where(has_right, gn_ref[...], 0.0)
    rb = 32
    for r in range(tm // rb):
        acc = jnp.broadcast_to(cb_ref[...], (rb, CONV_CH))
        for k in range(CONV_WIDTH):
            start = HALO - CONV_HALF + r * rb + k
            acc = acc + cw_ref[k:k + 1, :] * gbuf[start:start + rb, :]
        a = _ln(acc, clg_ref[...], clb_ref[...])
        abuf[r * rb:(r + 1) * rb, :] = (a * jax.nn.sigmoid(a)).astype(BF16)
    y = jnp.dot(abuf[...], wout_ref[:CONV_CH, :], preferred_element_type=F32)
    y = y + jnp.dot(bo_ref[...].astype(BF16), wout_ref[CONV_CH:, :], preferred_element_type=F32)
    o_ref[...] = x_ref[...] + gt_ref[...] * _rms(y, g3_ref[...])


def _cm_out(lay, x, mod_l, norm_w, g, b_out, conv_w, conv_b, cln_g, cln_b, w_out, l, j):
    tm = 256
    assert lay.prompt_len == tm and lay.dec_len % tm == 0
    hb = tm // HALO
    last = lay.t // HALO - 1
    tile = pl.BlockSpec((tm, D_MODEL), lambda i: (i, 0))
    half = pl.BlockSpec((tm, CONV_CH), lambda i: (i, 0))
    vec = pl.BlockSpec((None, 1, CONV_CH), lambda i: (j, 0, 0))
    return pl.pallas_call(
        functools.partial(_cm_out_kernel, tm=tm, p_tiles=lay.tp // tm, per_seq=lay.dec_len // tm),
        grid=(lay.t // tm,),
        in_specs=[tile, _mod_spec(lay, tm, 5), _vec_spec(l * 6 + 3),
                  pl.BlockSpec((HALO, CONV_CH), lambda i: (jnp.maximum(i * hb - 1, 0), 0)),
                  half,
                  pl.BlockSpec((HALO, CONV_CH), lambda i: (jnp.minimum((i + 1) * hb, last), 0)),
                  half,
                  pl.BlockSpec((None, CONV_WIDTH, CONV_CH), lambda i: (j, 0, 0)),
                  vec, vec, vec,
                  _resident((None, CONV_CH + SGU_CH, D_MODEL), (j, 0, 0))],
        out_specs=tile,
        out_shape=jax.ShapeDtypeStruct((lay.t, D_MODEL), F32),
        scratch_shapes=[pltpu.VMEM((tm + 2 * HALO, CONV_CH), F32),
                        pltpu.VMEM((tm, CONV_CH), BF16)],
        compiler_params=_params(32),
        name="conv_gmlp_out",
    )(x, mod_l, norm_w, g, g, g, b_out, conv_w, conv_b, cln_g, cln_b, w_out)


def _rope_tables(dec_len, pad_rows):
    pos = np.arange(dec_len)
    r = (pos // GRID_W).astype(np.float64)
    c = (pos % GRID_W).astype(np.float64)
    inv = np.power(ROPE_BASE, -np.arange(0, ROPE_HALF, 2, dtype=np.float64) / ROPE_HALF)
    ang_r = r[:, None] * inv[None, :]
    ang_c = c[:, None] * inv[None, :]
    cos = np.concatenate([np.cos(ang_r)] * 2 + [np.cos(ang_c)] * 2, axis=-1)
    sin = np.concatenate([-np.sin(ang_r), np.sin(ang_r), -np.sin(ang_c), np.sin(ang_c)], axis=-1)
    cos = np.concatenate([np.ones((pad_rows, HEAD_DIM)), cos], axis=0)
    sin = np.concatenate([np.zeros((pad_rows, HEAD_DIM)), sin], axis=0)
    return (jnp.asarray(np.tile(cos, (1, 2)), F32), jnp.asarray(np.tile(sin, (1, 2)), F32))


def _rope(z, cos, sin, first_half):
    partner = jnp.where(first_half, pltpu.roll(z, 128 - ROPE_HALF // 2, 1), pltpu.roll(z, ROPE_HALF // 2, 1))
    return z * cos + partner * sin


def _qkv_kernel(x_ref, sh_ref, sc_ref, g_ref, w_ref, cos_ref, sin_ref, q_ref, k_ref, v_ref, *, tm):
    x = x_ref[...]
    h = (_rms(x, g_ref[...]) * (1.0 + sc_ref[...]) + sh_ref[...]).astype(BF16)
    z = jnp.dot(h, w_ref[...], preferred_element_type=F32)
    cos = cos_ref[...]
    sin = sin_ref[...]
    lane = lax.broadcasted_iota(jnp.int32, (tm, 128), 1)
    first_half = (lane % ROPE_HALF) < (ROPE_HALF // 2)
    for c in range(Q_DIM // 128):
        zc = _rope(z[:, c * 128:(c + 1) * 128], cos, sin, first_half)
        q_ref[:, c * 128:(c + 1) * 128] = (zc * ATTN_SCALE).astype(BF16)
    for c in range(KV_DIM // 128):
        lo = Q_DIM + c * 128
        k_ref[:, c * 128:(c + 1) * 128] = _rope(z[:, lo:lo + 128], cos, sin, first_half)
    v_ref[...] = z[:, Q_DIM + KV_DIM:]


def _qkv(lay, x, mod_l, norm_w, w_qkv, cos_t, sin_t, l, j, tm):
    p_tiles = lay.tp // tm
    per_seq = lay.dec_len // tm
    tile = pl.BlockSpec((tm, D_MODEL), lambda i: (i, 0))
    tab = pl.BlockSpec((tm, 128), lambda i: (jnp.where(i < p_tiles, 0, 1 + (i - p_tiles) % per_seq), 0))
    kv = pl.BlockSpec((tm, KV_DIM), lambda i: (i, 0))
    return pl.pallas_call(
        functools.partial(_qkv_kernel, tm=tm),
        grid=(lay.t // tm,),
        in_specs=[tile, _mod_spec(lay, tm, 3), _mod_spec(lay, tm, 4), _vec_spec(l * 6 + 2),
                  _resident((None, D_MODEL, Q_DIM + 2 * KV_DIM), (j, 0, 0)), tab, tab],
        out_specs=[pl.BlockSpec((tm, Q_DIM), lambda i: (i, 0)), kv, kv],
        out_shape=[jax.ShapeDtypeStruct((lay.t, Q_DIM), BF16),
                   jax.ShapeDtypeStruct((lay.t, KV_DIM), F32),
                   jax.ShapeDtypeStruct((lay.t, KV_DIM), F32)],
        compiler_params=_params(40),
        name="qkv_rope",
    )(x, mod_l, mod_l, norm_w, w_qkv, cos_t, sin_t)


def _attend(q_ref, sink_ref, wo_ref, y_scr, keys, values, masks, j):
    rowblk = lax.broadcasted_iota(jnp.int32, (GROUP * BLOCK, 1), 0) // BLOCK
    y = jnp.zeros((BLOCK, D_MODEL), F32)
    for kvh in range(N_KV_HEADS):
        lanes = slice(kvh * HEAD_DIM, (kvh + 1) * HEAD_DIM)
        q = jnp.concatenate(
            [q_ref[:, (kvh * GROUP + g) * HEAD_DIM:(kvh * GROUP + g + 1) * HEAD_DIM] for g in range(GROUP)],
            axis=0)
        sink = jnp.zeros((GROUP * BLOCK, 1), F32)
        for g in range(GROUP):
            sink = jnp.where(rowblk == g, sink_ref[j, kvh * GROUP + g], sink)
        scores = []
        m = sink
        for k_ref, mask in zip(keys, masks):
            k = k_ref[:, lanes].astype(BF16)
            s = lax.dot_general(q, k, (((1,), (1,)), ((), ())), preferred_element_type=F32)
            if mask is not None:
                s = jnp.where(mask, s, NEG_INF)
            scores.append(s)
            m = jnp.maximum(m, jnp.max(s, axis=-1, keepdims=True))
        denom = jnp.exp(sink - m)
        o = jnp.zeros((GROUP * BLOCK, HEAD_DIM), F32)
        for s, v_ref in zip(scores, values):
            p = jnp.exp(s - m)
            denom = denom + jnp.sum(p, axis=-1, keepdims=True)
            o = o + jnp.dot(p.astype(BF16), v_ref[:, lanes].astype(BF16), preferred_element_type=F32)
        o = (o / denom).astype(BF16)
        for g in range(GROUP):
            h = kvh * GROUP + g
            y = y + jnp.dot(o[g * BLOCK:(g + 1) * BLOCK], wo_ref[h * HEAD_DIM:(h + 1) * HEAD_DIM, :],
                            preferred_element_type=F32)
    y_scr[...] = y


def _attn_kernel(sink_ref, x_ref, gt_ref, g3_ref, q_ref, kp_ref, kc_ref, kn_ref, vp_ref, vc_ref, vn_ref,
                 ck_ref, cv_ref, wo_ref, o_ref, y_scr, *, p_blocks, per_seq, j):
    i = pl.program_id(0)
    is_prompt = i < p_blocks

    @pl.when(is_prompt)
    def _():
        _attend(q_ref, sink_ref, wo_ref, y_scr, [kp_ref, kc_ref], [vp_ref, vc_ref], [None, None], j)

    @pl.when(jnp.logical_not(is_prompt))
    def _():
        local = (i - p_blocks) % per_seq
        r = lax.broadcasted_iota(jnp.int32, (GROUP * BLOCK, BLOCK), 0) % BLOCK
        c = lax.broadcasted_iota(jnp.int32, (GROUP * BLOCK, BLOCK), 1)
        prev_mask = jnp.logical_and(c >= r, local != 0)
        next_mask = jnp.logical_and(c <= r, local != per_seq - 1)
        _attend(q_ref, sink_ref, wo_ref, y_scr, [kp_ref, kc_ref, kn_ref, ck_ref],
                [vp_ref, vc_ref, vn_ref, cv_ref], [prev_mask, None, next_mask, None], j)

    o_ref[...] = x_ref[...] + gt_ref[...] * _rms(y_scr[...], g3_ref[...])


def _attention(lay, x, mod_l, norm_w, q, k, v, cache_k, cache_v, sink, w_o, l, j):
    tm = BLOCK
    assert lay.prompt_len == 2 * BLOCK and lay.dec_len % BLOCK == 0
    p_blocks = lay.tp // tm
    per_seq = lay.dec_len // tm

    def prev_idx(i):
        local = (i - p_blocks) % per_seq
        return jnp.where(i < p_blocks, i ^ 1, jnp.where(local == 0, i, i - 1))

    def next_idx(i):
        local = (i - p_blocks) % per_seq
        return jnp.where(i < p_blocks, i, jnp.where(local == per_seq - 1, i, i + 1))

    def cache_idx(i):
        return jnp.where(i < p_blocks, 0, (i - p_blocks) // per_seq)

    tile = pl.BlockSpec((tm, D_MODEL), lambda i: (i, 0))
    kv_prev = pl.BlockSpec((tm, KV_DIM), lambda i: (prev_idx(i), 0))
    kv_cur = pl.BlockSpec((tm, KV_DIM), lambda i: (i, 0))
    kv_next = pl.BlockSpec((tm, KV_DIM), lambda i: (next_idx(i), 0))
    past = cache_k.shape[2]
    ctx = pl.BlockSpec((None, None, past, KV_DIM), lambda i: (cache_idx(i), j, 0, 0))
    return pl.pallas_call(
        functools.partial(_attn_kernel, p_blocks=p_blocks, per_seq=per_seq, j=j),
        grid=(lay.t // tm,),
        in_specs=[pl.BlockSpec(memory_space=pltpu.SMEM),
                  tile, _mod_spec(lay, tm, 5), _vec_spec(l * 6 + 3),
                  pl.BlockSpec((tm, Q_DIM), lambda i: (i, 0)),
                  kv_prev, kv_cur, kv_next, kv_prev, kv_cur, kv_next, ctx, ctx,
                  _resident((None, Q_DIM, D_MODEL), (j, 0, 0))],
        out_specs=tile,
        out_shape=jax.ShapeDtypeStruct((lay.t, D_MODEL), F32),
        scratch_shapes=[pltpu.VMEM((tm, D_MODEL), F32)],
        compiler_params=_params(32),
        name="window_attention",
    )(sink, x, mod_l, norm_w, q, k, k, k, v, v, v, cache_k, cache_v, w_o)


def kernel(x_prompt, x_sample, cache_k, cache_v, c, c_ctx, w_mod, b_mod, norm_w, ffn_w_gate, ffn_w_up,
           ffn_w_down, cm_w_in, cm_conv_w, cm_conv_b, cm_conv_ln_g, cm_conv_ln_b, cm_sgu_ln_g, cm_sgu_ln_b,
           cm_sgu_w, cm_sgu_b, cm_w_out, attn_w_qkv, attn_w_o, attn_sink):
    n_prompt, prompt_len, _ = x_prompt.shape
    n_dec, dec_len, _ = x_sample.shape
    lay = _Layout(n_prompt, prompt_len, n_dec, dec_len)
    n_attn = attn_w_qkv.shape[0]
    n_cm = cm_w_in.shape[0]
    past = cache_k.shape[2]

    x = jnp.concatenate([x_prompt.reshape(lay.tp, D_MODEL), x_sample.reshape(lay.ts, D_MODEL)], axis=0)
    cond = jnp.concatenate([c_ctx[None, :], c, jnp.zeros((MOD_ROWS - 1 - n_dec, D_MODEL), F32)], axis=0)
    mod = _modulation(cond, w_mod, b_mod).reshape(DEPTH, MOD_ROWS, 1, N_MOD * D_MODEL)
    norm_rows = norm_w.reshape(DEPTH * 6, 1, D_MODEL)

    wg = ffn_w_gate.astype(BF16)
    wu = ffn_w_up.astype(BF16)
    wd = ffn_w_down.astype(BF16)
    w_in = cm_w_in.astype(BF16)
    w_out = cm_w_out.astype(BF16)
    sgu_w = cm_sgu_w.astype(BF16)
    sgu_bt = jnp.swapaxes(cm_sgu_b, 1, 2)
    w_qkv = attn_w_qkv.astype(BF16)
    w_o = attn_w_o.astype(BF16)
    vec_cm = lambda a: a.reshape(n_cm, 1, -1)
    ck = cache_k.reshape(n_dec, n_attn, past, KV_DIM)
    cv = cache_v.reshape(n_dec, n_attn, past, KV_DIM)
    qkv_tm = 512
    cos_t, sin_t = _rope_tables(dec_len, qkv_tm)

    new_k, new_v = [], []
    for l in range(DEPTH):
        mod_l = mod[l]
        j = l // 2
        x = _ffn(lay, x, mod_l, norm_rows, wg, wu, wd, l, 0, 0)
        if l % 2 == 0:
            g, b_out = _cm_in(lay, x, mod_l, norm_rows, w_in, vec_cm(cm_sgu_ln_g), vec_cm(cm_sgu_ln_b),
                              sgu_w, sgu_bt, l, j)
            x = _cm_out(lay, x, mod_l, norm_rows, g, b_out, cm_conv_w, vec_cm(cm_conv_b),
                        vec_cm(cm_conv_ln_g), vec_cm(cm_conv_ln_b), w_out, l, j)
        else:
            q, k, v = _qkv(lay, x, mod_l, norm_rows, w_qkv, cos_t, sin_t, l, j, qkv_tm)
            new_k.append(k[:lay.tp].reshape(n_prompt, prompt_len, N_KV_HEADS, HEAD_DIM))
            new_v.append(v[:lay.tp].reshape(n_prompt, prompt_len, N_KV_HEADS, HEAD_DIM))
            x = _attention(lay, x, mod_l, norm_rows, q, k, v, ck, cv, attn_sink, w_o, l, j)
        x = _ffn(lay, x, mod_l, norm_rows, wg, wu, wd, l, 1, 2)
    y_prompt = x[:lay.tp].reshape(x_prompt.shape)
    y_sample = x[lay.tp:].reshape(x_sample.shape)
    return y_prompt, y_sample, jnp.stack(new_k, axis=1), jnp.stack(new_v, axis=1)
```

```python
import functools

import numpy as np
import jax
import jax.numpy as jnp
from jax import lax
from jax.experimental import pallas as pl
from jax.experimental.pallas import tpu as pltpu

F32 = jnp.float32
BF16 = jnp.bfloat16

D_MODEL = 1024
DEPTH = 4
GRID_W = 64
N_MOD = 9
D_FF = 2816
CONV_CH = 512
CONV_WIDTH = 31
CONV_HALF = CONV_WIDTH // 2
SGU_CH = 512
SGU_GROUPS = 4
CHUNK = 128
HEAD_DIM = 64
N_HEADS = 16
N_KV_HEADS = 4
GROUP = N_HEADS // N_KV_HEADS
Q_DIM = N_HEADS * HEAD_DIM
KV_DIM = N_KV_HEADS * HEAD_DIM
BLOCK = 128
ROPE_HALF = HEAD_DIM // 2
ROPE_BASE = 10000.0
EPS = 1e-6
NEG_INF = -1e30
ATTN_SCALE = HEAD_DIM ** -0.5
LOG2E = float(np.log2(np.e))

SUBLANES = 8
MOD_ROWS = 8
HALO = 16
MIB = 1024 * 1024


def _rms(x, g):
    return x * lax.rsqrt(jnp.mean(x * x, axis=-1, keepdims=True) + EPS) * g


def _ln(x, g, b):
    mu = jnp.mean(x, axis=-1, keepdims=True)
    xc = x - mu
    var = jnp.mean(xc * xc, axis=-1, keepdims=True)
    return xc * lax.rsqrt(var + EPS) * g + b


def _resident(block_shape, index):
    return pl.BlockSpec(block_shape, lambda i: index, pipeline_mode=pl.Buffered(1))


def _params(vmem_mib):
    return pltpu.CompilerParams(dimension_semantics=("arbitrary",),
                                vmem_limit_bytes=vmem_mib * MIB)


class _Layout:
    def __init__(self, n_prompt, prompt_len, n_dec, dec_len):
        self.n_prompt, self.prompt_len = n_prompt, prompt_len
        self.n_dec, self.dec_len = n_dec, dec_len
        self.tp = n_prompt * prompt_len
        self.ts = n_dec * dec_len
        self.t = self.tp + self.ts

    def mod_row(self, i, tm):
        p_tiles = self.tp // tm
        per_seq = self.dec_len // tm
        return jnp.where(i < p_tiles, 0, 1 + (i - p_tiles) // per_seq)


def _vec_spec(row):
    return pl.BlockSpec((None, 1, D_MODEL), lambda i: (row, 0, 0))


def _mod_spec(lay, tm, col):
    return pl.BlockSpec((None, 1, D_MODEL), lambda i: (lay.mod_row(i, tm), 0, col))


def _mod_kernel(c_ref, w_ref, b_ref, o_ref):
    c = c_ref[...]
    s = (c * jax.nn.sigmoid(c)).astype(BF16)
    o_ref[...] = jnp.dot(s, w_ref[...].astype(BF16), preferred_element_type=F32) + b_ref[...]


def _modulation(cond, w_mod, b_mod):
    tn = 1024
    n = N_MOD * D_MODEL
    return pl.pallas_call(
        _mod_kernel,
        grid=(DEPTH, n // tn),
        in_specs=[pl.BlockSpec((MOD_ROWS, D_MODEL), lambda l, j: (0, 0)),
                  pl.BlockSpec((None, D_MODEL, tn), lambda l, j: (l, 0, j)),
                  pl.BlockSpec((None, 1, tn), lambda l, j: (l, 0, j))],
        out_specs=pl.BlockSpec((None, MOD_ROWS, tn), lambda l, j: (l, 0, j)),
        out_shape=jax.ShapeDtypeStruct((DEPTH, MOD_ROWS, n), F32),
        compiler_params=pltpu.CompilerParams(dimension_semantics=("arbitrary", "arbitrary"),
                                             vmem_limit_bytes=24 * MIB),
        name="modulation",
    )(cond, w_mod, b_mod.reshape(DEPTH, 1, n))


def _ffn_kernel(*refs, weight, p_tiles, split_in, split_out, sub):
    n_x = 2 if split_in else 1
    x_refs, o_refs = refs[:n_x], refs[n_x + 8:]
    sh_ref, sc_ref, gt_ref, g1_ref, g2_ref, wg_ref, wu_ref, wd_ref = refs[n_x:n_x + 8]
    is_prompt = pl.program_id(0) < p_tiles
    tm = x_refs[0].shape[0]
    for r in range(tm // sub):
        rows = slice(r * sub, (r + 1) * sub)
        if split_in:
            x = jnp.where(is_prompt, x_refs[0][rows, :], x_refs[1][rows, :])
        else:
            x = x_refs[0][rows, :]
        h = (_rms(x, g1_ref[...]) * (1.0 + sc_ref[...]) + sh_ref[...]).astype(BF16)
        a = jnp.dot(h, wg_ref[...], preferred_element_type=F32)
        b = jnp.dot(h, wu_ref[...], preferred_element_type=F32)
        act = (a * jax.nn.sigmoid(a) * b).astype(BF16)
        y = jnp.dot(act, wd_ref[...], preferred_element_type=F32)
        out = x + weight * gt_ref[...] * _rms(y, g2_ref[...])
        if split_out:
            @pl.when(is_prompt)
            def _():
                o_refs[0][rows, :] = out

            @pl.when(jnp.logical_not(is_prompt))
            def _():
                o_refs[1][rows, :] = out
        else:
            o_refs[0][rows, :] = out


def _ffn(lay, xs, mod_l, norm_w, wg, wu, wd, l, s, slot, split_out=False):
    tm = 512
    p_tiles = lay.tp // tm
    tile = pl.BlockSpec((tm, D_MODEL), lambda i: (i, 0))
    p_tile = pl.BlockSpec((tm, D_MODEL), lambda i: (jnp.minimum(i, p_tiles - 1), 0))
    s_tile = pl.BlockSpec((tm, D_MODEL), lambda i: (jnp.maximum(i - p_tiles, 0), 0))
    split_in = isinstance(xs, tuple)
    xs = xs if split_in else (xs,)
    if split_out:
        out_specs = [p_tile, s_tile]
        out_shape = [jax.ShapeDtypeStruct((lay.tp, D_MODEL), F32), jax.ShapeDtypeStruct((lay.ts, D_MODEL), F32)]
    else:
        out_specs, out_shape = tile, jax.ShapeDtypeStruct((lay.t, D_MODEL), F32)
    return pl.pallas_call(
        functools.partial(_ffn_kernel, weight=0.5, p_tiles=p_tiles, split_in=split_in, split_out=split_out,
                          sub=256),
        grid=(lay.t // tm,),
        in_specs=([p_tile, s_tile] if split_in else [tile]) + [
            _mod_spec(lay, tm, 3 * slot), _mod_spec(lay, tm, 3 * slot + 1), _mod_spec(lay, tm, 3 * slot + 2),
            _vec_spec(l * 6 + 2 * slot), _vec_spec(l * 6 + 2 * slot + 1),
            _resident((None, None, D_MODEL, D_FF), (l, s, 0, 0)),
            _resident((None, None, D_MODEL, D_FF), (l, s, 0, 0)),
            _resident((None, None, D_FF, D_MODEL), (l, s, 0, 0))],
        out_specs=out_specs,
        out_shape=out_shape,
        compiler_params=_params(52),
        name="swiglu",
    )(*xs, mod_l, mod_l, mod_l, norm_w, norm_w, wg, wu, wd)


def _cm_in_kernel(x_ref, sh_ref, sc_ref, g_ref, win_ref, slg_ref, slb_ref, sw_ref, sb_ref,
                  gout_ref, bout_ref, *, tm):
    x = x_ref[...]
    h = (_rms(x, g_ref[...]) * (1.0 + sc_ref[...]) + sh_ref[...]).astype(BF16)
    z = jnp.dot(h, win_ref[...], preferred_element_type=F32)
    a_val = z[:, :CONV_CH]
    a_gate = z[:, CONV_CH:2 * CONV_CH]
    u = jax.nn.gelu(z[:, 2 * CONV_CH:2 * CONV_CH + SGU_CH])
    v = _ln(jax.nn.gelu(z[:, 2 * CONV_CH + SGU_CH:]), slg_ref[...], slb_ref[...]).astype(BF16)
    gout_ref[...] = a_val * jax.nn.sigmoid(a_gate)
    gc = SGU_CH // SGU_GROUPS
    for n in range(tm // CHUNK):
        rows = slice(n * CHUNK, (n + 1) * CHUNK)
        for gi in range(SGU_GROUPS):
            cols = slice(gi * gc, (gi + 1) * gc)
            vc = jnp.dot(sw_ref[gi], v[rows, cols], preferred_element_type=F32) + sb_ref[:, gi:gi + 1]
            bout_ref[rows, cols] = u[rows, cols] * vc


def _cm_in(lay, x, mod_l, norm_w, w_in, sln_g, sln_b, sgu_w, sgu_bt, l, j):
    tm = 256
    tile = pl.BlockSpec((tm, D_MODEL), lambda i: (i, 0))
    half = pl.BlockSpec((tm, CONV_CH), lambda i: (i, 0))
    n_in = 2 * CONV_CH + 2 * SGU_CH
    return pl.pallas_call(
        functools.partial(_cm_in_kernel, tm=tm),
        grid=(lay.t // tm,),
        in_specs=[tile, _mod_spec(lay, tm, 3), _mod_spec(lay, tm, 4), _vec_spec(l * 6 + 2),
                  _resident((None, D_MODEL, n_in), (j, 0, 0)),
                  pl.BlockSpec((None, 1, SGU_CH), lambda i: (j, 0, 0)),
                  pl.BlockSpec((None, 1, SGU_CH), lambda i: (j, 0, 0)),
                  _resident((None, SGU_GROUPS, CHUNK, CHUNK), (j, 0, 0, 0)),
                  pl.BlockSpec((None, CHUNK, SGU_GROUPS), lambda i: (j, 0, 0))],
        out_specs=[half, half],
        out_shape=[jax.ShapeDtypeStruct((lay.t, CONV_CH), F32),
                   jax.ShapeDtypeStruct((lay.t, SGU_CH), F32)],
        compiler_params=_params(40),
        name="conv_gmlp_in",
    )(x, mod_l, mod_l, norm_w, w_in, sln_g, sln_b, sgu_w, sgu_bt)


def _cm_out_kernel(x_ref, gt_ref, g3_ref, gp_ref, gc_ref, gn_ref, bo_ref, cw_ref, cb_ref,
                   clg_ref, clb_ref, wout_ref, o_ref, gbuf, sbuf, abuf, *, tm, p_tiles, per_seq):
    i = pl.program_id(0)
    local = (i - p_tiles) % per_seq
    has_left = jnp.logical_and(i >= p_tiles, local != 0)
    has_right = jnp.logical_and(i >= p_tiles, local != per_seq - 1)
    gbuf[0:HALO, :] = jnp.where(has_left, gp_ref[...], 0.0)
    gbuf[HALO:HALO + tm, :] = gc_ref[...]
    gbuf[HALO + tm:, :] = jnp.where(has_right, gn_ref[...], 0.0)
    span = tm + 2 * HALO - SUBLANES
    for s in range(1, SUBLANES):
        sbuf[s - 1] = gbuf[s:s + span, :]
    rb = 32
    for r in range(tm // rb):
        acc = jnp.broadcast_to(cb_ref[...], (rb // SUBLANES, SUBLANES, CONV_CH))
        for k in range(CONV_WIDTH):
            off = HALO - CONV_HALF + k
            s, base = off % SUBLANES, r * rb + off - off % SUBLANES
            src = gbuf[base:base + rb, :] if s == 0 else sbuf[s - 1, base:base + rb, :]
            acc = acc + cw_ref[k][None] * src.reshape(rb // SUBLANES, SUBLANES, CONV_CH)
        a = _ln(acc.reshape(rb, CONV_CH), clg_ref[...], clb_ref[...])
        abuf[r * rb:(r + 1) * rb, :] = (a * jax.nn.sigmoid(a)).astype(BF16)
    y = jnp.dot(abuf[...], wout_ref[:CONV_CH, :], preferred_element_type=F32)
    y = y + jnp.dot(bo_ref[...].astype(BF16), wout_ref[CONV_CH:, :], preferred_element_type=F32)
    o_ref[...] = x_ref[...] + gt_ref[...] * _rms(y, g3_ref[...])


def _cm_out(lay, x, mod_l, norm_w, g, b_out, conv_w, conv_b, cln_g, cln_b, w_out, l, j):
    tm = 256
    assert lay.prompt_len == tm and lay.dec_len % tm == 0
    hb = tm // HALO
    last = lay.t // HALO - 1
    tile = pl.BlockSpec((tm, D_MODEL), lambda i: (i, 0))
    half = pl.BlockSpec((tm, CONV_CH), lambda i: (i, 0))
    vec = pl.BlockSpec((None, 1, CONV_CH), lambda i: (j, 0, 0))
    return pl.pallas_call(
        functools.partial(_cm_out_kernel, tm=tm, p_tiles=lay.tp // tm, per_seq=lay.dec_len // tm),
        grid=(lay.t // tm,),
        in_specs=[tile, _mod_spec(lay, tm, 5), _vec_spec(l * 6 + 3),
                  pl.BlockSpec((HALO, CONV_CH), lambda i: (jnp.maximum(i * hb - 1, 0), 0)),
                  half,
                  pl.BlockSpec((HALO, CONV_CH), lambda i: (jnp.minimum((i + 1) * hb, last), 0)),
                  half,
                  pl.BlockSpec((None, CONV_WIDTH, SUBLANES, CONV_CH), lambda i: (j, 0, 0, 0)),
                  vec, vec, vec,
                  _resident((None, CONV_CH + SGU_CH, D_MODEL), (j, 0, 0))],
        out_specs=tile,
        out_shape=jax.ShapeDtypeStruct((lay.t, D_MODEL), F32),
        scratch_shapes=[pltpu.VMEM((tm + 2 * HALO, CONV_CH), F32),
                        pltpu.VMEM((SUBLANES - 1, tm + 2 * HALO - SUBLANES, CONV_CH), F32),
                        pltpu.VMEM((tm, CONV_CH), BF16)],
        compiler_params=_params(32),
        name="conv_gmlp_out",
    )(x, mod_l, norm_w, g, g, g, b_out, conv_w, conv_b, cln_g, cln_b, w_out)


def _rope_tables(dec_len, pad_rows):
    pos = np.arange(dec_len)
    r = (pos // GRID_W).astype(np.float64)
    c = (pos % GRID_W).astype(np.float64)
    inv = np.power(ROPE_BASE, -np.arange(0, ROPE_HALF, 2, dtype=np.float64) / ROPE_HALF)
    ang_r = r[:, None] * inv[None, :]
    ang_c = c[:, None] * inv[None, :]
    cos = np.concatenate([np.cos(ang_r)] * 2 + [np.cos(ang_c)] * 2, axis=-1)
    sin = np.concatenate([-np.sin(ang_r), np.sin(ang_r), -np.sin(ang_c), np.sin(ang_c)], axis=-1)
    cos = np.concatenate([np.ones((pad_rows, HEAD_DIM)), cos], axis=0)
    sin = np.concatenate([np.zeros((pad_rows, HEAD_DIM)), sin], axis=0)
    return (jnp.asarray(np.tile(cos, (1, 2)), F32), jnp.asarray(np.tile(sin, (1, 2)), F32))


def _rope(z, cos, sin, first_half):
    partner = jnp.where(first_half, pltpu.roll(z, 128 - ROPE_HALF // 2, 1), pltpu.roll(z, ROPE_HALF // 2, 1))
    return z * cos + partner * sin


def _qkv_kernel(x_ref, sh_ref, sc_ref, g_ref, w_ref, cos_ref, sin_ref, q_ref, k_ref, v_ref, kb_ref, vb_ref,
                *, tm):
    x = x_ref[...]
    h = (_rms(x, g_ref[...]) * (1.0 + sc_ref[...]) + sh_ref[...]).astype(BF16)
    z = jnp.dot(h, w_ref[...], preferred_element_type=F32)
    cos = cos_ref[...]
    sin = sin_ref[...]
    lane = lax.broadcasted_iota(jnp.int32, (tm, 128), 1)
    first_half = (lane % ROPE_HALF) < (ROPE_HALF // 2)
    for c in range(Q_DIM // 128):
        zc = _rope(z[:, c * 128:(c + 1) * 128], cos, sin, first_half)
        q_ref[:, c * 128:(c + 1) * 128] = (zc * (ATTN_SCALE * LOG2E)).astype(BF16)
    for c in range(KV_DIM // 128):
        lo = Q_DIM + c * 128
        kc = _rope(z[:, lo:lo + 128], cos, sin, first_half)
        k_ref[:, c * 128:(c + 1) * 128] = kc
        kb_ref[:, c * 128:(c + 1) * 128] = kc.astype(BF16)
    v = z[:, Q_DIM + KV_DIM:]
    v_ref[...] = v
    vb_ref[...] = v.astype(BF16)


def _qkv(lay, x, mod_l, norm_w, w_qkv, cos_t, sin_t, l, j, tm):
    p_tiles = lay.tp // tm
    per_seq = lay.dec_len // tm
    tile = pl.BlockSpec((tm, D_MODEL), lambda i: (i, 0))
    tab = pl.BlockSpec((tm, 128), lambda i: (jnp.where(i < p_tiles, 0, 1 + (i - p_tiles) % per_seq), 0))
    kv = pl.BlockSpec((tm, KV_DIM), lambda i: (i, 0))
    return pl.pallas_call(
        functools.partial(_qkv_kernel, tm=tm),
        grid=(lay.t // tm,),
        in_specs=[tile, _mod_spec(lay, tm, 3), _mod_spec(lay, tm, 4), _vec_spec(l * 6 + 2),
                  _resident((None, D_MODEL, Q_DIM + 2 * KV_DIM), (j, 0, 0)), tab, tab],
        out_specs=[pl.BlockSpec((tm, Q_DIM), lambda i: (i, 0)), kv, kv, kv, kv],
        out_shape=[jax.ShapeDtypeStruct((lay.t, Q_DIM), BF16),
                   jax.ShapeDtypeStruct((lay.t, KV_DIM), F32),
                   jax.ShapeDtypeStruct((lay.t, KV_DIM), F32),
                   jax.ShapeDtypeStruct((lay.t, KV_DIM), BF16),
                   jax.ShapeDtypeStruct((lay.t, KV_DIM), BF16)],
        compiler_params=_params(40),
        name="qkv_rope",
    )(x, mod_l, mod_l, norm_w, w_qkv, cos_t, sin_t)


def _attend(q_ref, sink_ref, o_scr, keys, values, masks, j):
    rowblk = lax.broadcasted_iota(jnp.int32, (GROUP * BLOCK, 1), 0) // BLOCK
    low_lanes = lax.broadcasted_iota(jnp.int32, (BLOCK, 2 * HEAD_DIM), 1) < HEAD_DIM
    for kvh in range(N_KV_HEADS):
        lanes = slice(kvh * HEAD_DIM, (kvh + 1) * HEAD_DIM)
        pair = slice((kvh // 2) * 2 * HEAD_DIM, (kvh // 2 + 1) * 2 * HEAD_DIM)
        q = jnp.concatenate(
            [q_ref[:, (kvh * GROUP + g) * HEAD_DIM:(kvh * GROUP + g + 1) * HEAD_DIM] for g in range(GROUP)],
            axis=0)
        sink = jnp.zeros((GROUP * BLOCK, 1), F32)
        for g in range(GROUP):
            sink = jnp.where(rowblk == g, sink_ref[j, kvh * GROUP + g] * LOG2E, sink)
        scores = []
        for k_ref, mask in zip(keys, masks):
            s = lax.dot_general(q, k_ref[:, lanes], (((1,), (1,)), ((), ())), preferred_element_type=F32)
            if mask is not None:
                s = jnp.where(mask, s, NEG_INF)
            scores.append(s)
        cols = [s[:, c * BLOCK:(c + 1) * BLOCK] for s in scores for c in range(s.shape[1] // BLOCK)]
        m = jnp.maximum(jnp.max(functools.reduce(jnp.maximum, cols), axis=-1, keepdims=True), sink)
        psum = jnp.zeros((GROUP * BLOCK, BLOCK), F32)
        o2 = jnp.zeros((GROUP * BLOCK, 2 * HEAD_DIM), F32)
        for s, v_ref in zip(scores, values):
            p = jnp.exp2(s - m)
            for c in range(s.shape[1] // BLOCK):
                psum = psum + p[:, c * BLOCK:(c + 1) * BLOCK]
            o2 = o2 + jnp.dot(p.astype(BF16), v_ref[:, pair], preferred_element_type=F32)
        denom = jnp.sum(psum, axis=-1, keepdims=True) + jnp.exp2(sink - m)
        o2 = o2 * (1.0 / denom)
        for half in range(GROUP // 2):
            a = o2[(2 * half) * BLOCK:(2 * half + 1) * BLOCK]
            b = o2[(2 * half + 1) * BLOCK:(2 * half + 2) * BLOCK]
            if kvh % 2 == 0:
                b = pltpu.roll(b, HEAD_DIM, 1)
            else:
                a = pltpu.roll(a, HEAD_DIM, 1)
            c = kvh * (GROUP // 2) + half
            o_scr[:, c * 2 * HEAD_DIM:(c + 1) * 2 * HEAD_DIM] = jnp.where(low_lanes, a, b).astype(BF16)


def _attn_kernel(sink_ref, x_ref, gt_ref, g3_ref, q_ref, kp_ref, kc_ref, kn_ref, vp_ref, vc_ref, vn_ref,
                 ck_ref, cv_ref, wo_ref, o_ref, o_scr, *, p_blocks, per_seq, j):
    i = pl.program_id(0)
    is_prompt = i < p_blocks

    @pl.when(is_prompt)
    def _():
        _attend(q_ref, sink_ref, o_scr, [kp_ref, kc_ref], [vp_ref, vc_ref], [None, None], j)

    @pl.when(jnp.logical_not(is_prompt))
    def _():
        local = (i - p_blocks) % per_seq
        r = lax.broadcasted_iota(jnp.int32, (GROUP * BLOCK, BLOCK), 0) % BLOCK
        c = lax.broadcasted_iota(jnp.int32, (GROUP * BLOCK, BLOCK), 1)
        prev_mask = jnp.logical_and(c >= r, local != 0)
        next_mask = jnp.logical_and(c <= r, local != per_seq - 1)
        _attend(q_ref, sink_ref, o_scr, [kp_ref, kc_ref, kn_ref, ck_ref],
                [vp_ref, vc_ref, vn_ref, cv_ref], [prev_mask, None, next_mask, None], j)

    y = jnp.dot(o_scr[...], wo_ref[...], preferred_element_type=F32)
    o_ref[...] = x_ref[...] + gt_ref[...] * _rms(y, g3_ref[...])


def _attention(lay, x, mod_l, norm_w, q, k, v, cache_k, cache_v, sink, w_o, l, j):
    tm = BLOCK
    assert lay.prompt_len == 2 * BLOCK and lay.dec_len % BLOCK == 0
    p_blocks = lay.tp // tm
    per_seq = lay.dec_len // tm

    def prev_idx(i):
        local = (i - p_blocks) % per_seq
        return jnp.where(i < p_blocks, i ^ 1, jnp.where(local == 0, i, i - 1))

    def next_idx(i):
        local = (i - p_blocks) % per_seq
        return jnp.where(i < p_blocks, i, jnp.where(local == per_seq - 1, i, i + 1))

    def cache_idx(i):
        return jnp.where(i < p_blocks, 0, (i - p_blocks) // per_seq)

    tile = pl.BlockSpec((tm, D_MODEL), lambda i: (i, 0))
    kv_prev = pl.BlockSpec((tm, KV_DIM), lambda i: (prev_idx(i), 0))
    kv_cur = pl.BlockSpec((tm, KV_DIM), lambda i: (i, 0))
    kv_next = pl.BlockSpec((tm, KV_DIM), lambda i: (next_idx(i), 0))
    past = cache_k.shape[2]
    ctx = pl.BlockSpec((None, None, past, KV_DIM), lambda i: (cache_idx(i), j, 0, 0))
    return pl.pallas_call(
        functools.partial(_attn_kernel, p_blocks=p_blocks, per_seq=per_seq, j=j),
        grid=(lay.t // tm,),
        in_specs=[pl.BlockSpec(memory_space=pltpu.SMEM),
                  tile, _mod_spec(lay, tm, 5), _vec_spec(l * 6 + 3),
                  pl.BlockSpec((tm, Q_DIM), lambda i: (i, 0)),
                  kv_prev, kv_cur, kv_next, kv_prev, kv_cur, kv_next, ctx, ctx,
                  _resident((None, Q_DIM, D_MODEL), (j, 0, 0))],
        out_specs=tile,
        out_shape=jax.ShapeDtypeStruct((lay.t, D_MODEL), F32),
        scratch_shapes=[pltpu.VMEM((tm, Q_DIM), BF16)],
        compiler_params=_params(32),
        name="window_attention",
    )(sink, x, mod_l, norm_w, q, k, k, k, v, v, v, cache_k, cache_v, w_o)


def kernel(x_prompt, x_sample, cache_k, cache_v, c, c_ctx, w_mod, b_mod, norm_w, ffn_w_gate, ffn_w_up,
           ffn_w_down, cm_w_in, cm_conv_w, cm_conv_b, cm_conv_ln_g, cm_conv_ln_b, cm_sgu_ln_g, cm_sgu_ln_b,
           cm_sgu_w, cm_sgu_b, cm_w_out, attn_w_qkv, attn_w_o, attn_sink):
    n_prompt, prompt_len, _ = x_prompt.shape
    n_dec, dec_len, _ = x_sample.shape
    lay = _Layout(n_prompt, prompt_len, n_dec, dec_len)
    n_attn = attn_w_qkv.shape[0]
    n_cm = cm_w_in.shape[0]
    past = cache_k.shape[2]

    x = (x_prompt.reshape(lay.tp, D_MODEL), x_sample.reshape(lay.ts, D_MODEL))
    cond =jnp.concatenate([c_ctx[None, :], c, jnp.zeros((MOD_ROWS - 1 - n_dec, D_MODEL), F32)], axis=0)
    mod = _modulation(cond, w_mod, b_mod).reshape(DEPTH, MOD_ROWS, 1, N_MOD * D_MODEL)
    norm_rows = norm_w.reshape(DEPTH * 6, 1, D_MODEL)

    wg = ffn_w_gate.astype(BF16)
    wu = ffn_w_up.astype(BF16)
    wd = ffn_w_down.astype(BF16)
    w_in = cm_w_in.astype(BF16)
    w_out = cm_w_out.astype(BF16)
    sgu_w = cm_sgu_w.astype(BF16)
    sgu_bt = jnp.swapaxes(cm_sgu_b, 1, 2)
    w_qkv = attn_w_qkv.astype(BF16)
    w_o = attn_w_o.astype(BF16)
    vec_cm = lambda a: a.reshape(n_cm, 1, -1)
    conv_w = jnp.broadcast_to(cm_conv_w[:, :, None, :], (n_cm, CONV_WIDTH, SUBLANES, CONV_CH))
    ck = cache_k.reshape(n_dec, n_attn, past, KV_DIM).astype(BF16)
    cv = cache_v.reshape(n_dec, n_attn, past, KV_DIM).astype(BF16)
    qkv_tm = 512
    cos_t, sin_t = _rope_tables(dec_len, qkv_tm)

    new_k, new_v = [], []
    for l in range(DEPTH):
        mod_l = mod[l]
        j = l // 2
        x = _ffn(lay, x, mod_l, norm_rows, wg, wu, wd, l, 0, 0)
        if l % 2 == 0:
            g, b_out = _cm_in(lay, x, mod_l, norm_rows, w_in, vec_cm(cm_sgu_ln_g), vec_cm(cm_sgu_ln_b),
                              sgu_w, sgu_bt, l, j)
            x = _cm_out(lay, x, mod_l, norm_rows, g, b_out, conv_w, vec_cm(cm_conv_b),
                        vec_cm(cm_conv_ln_g), vec_cm(cm_conv_ln_b), w_out, l, j)
        else:
            q, k, v, kb, vb = _qkv(lay, x, mod_l, norm_rows, w_qkv, cos_t, sin_t, l, j, qkv_tm)
            new_k.append(k[:lay.tp].reshape(n_prompt, prompt_len, N_KV_HEADS, HEAD_DIM))
            new_v.append(v[:lay.tp].reshape(n_prompt, prompt_len, N_KV_HEADS, HEAD_DIM))
            x = _attention(lay, x, mod_l, norm_rows, q, kb, vb, ck, cv, attn_sink, w_o, l, j)
        x = _ffn(lay, x, mod_l, norm_rows, wg, wu, wd, l, 1, 2, split_out=(l == DEPTH - 1))
    y_prompt, y_sample = x
    return (y_prompt.reshape(x_prompt.shape), y_sample.reshape(x_sample.shape),
            jnp.stack(new_k, axis=1), jnp.stack(new_v, axis=1))
```

```python
import functools

import numpy as np
import jax
import jax.numpy as jnp
from jax import lax
from jax.experimental import pallas as pl
from jax.experimental.pallas import tpu as pltpu

F32 = jnp.float32
BF16 = jnp.bfloat16

D_MODEL = 1024
DEPTH = 4
GRID_W = 64
N_MOD = 9
D_FF = 2816
CONV_CH = 512
CONV_WIDTH = 31
CONV_HALF = CONV_WIDTH // 2
SGU_CH = 512
SGU_GROUPS = 4
CHUNK = 128
HEAD_DIM = 64
N_HEADS = 16
N_KV_HEADS = 4
GROUP = N_HEADS // N_KV_HEADS
Q_DIM = N_HEADS * HEAD_DIM
KV_DIM = N_KV_HEADS * HEAD_DIM
BLOCK = 128
ROPE_HALF = HEAD_DIM // 2
ROPE_BASE = 10000.0
EPS = 1e-6
NEG_INF = -1e30
ATTN_SCALE = HEAD_DIM ** -0.5
LOG2E = float(np.log2(np.e))

SUBLANES = 8
MOD_ROWS = 8
HALO = 16
MIB = 1024 * 1024


def _rms(x, g):
    return x * lax.rsqrt(jnp.mean(x * x, axis=-1, keepdims=True) + EPS) * g


def _ln(x, g, b):
    mu = jnp.mean(x, axis=-1, keepdims=True)
    xc = x - mu
    var = jnp.mean(xc * xc, axis=-1, keepdims=True)
    return xc * lax.rsqrt(var + EPS) * g + b


def _resident(block_shape, index):
    return pl.BlockSpec(block_shape, lambda i: index, pipeline_mode=pl.Buffered(1))


def _params(vmem_mib):
    return pltpu.CompilerParams(dimension_semantics=("arbitrary",),
                                vmem_limit_bytes=vmem_mib * MIB)


class _Layout:
    def __init__(self, n_prompt, prompt_len, n_dec, dec_len):
        self.n_prompt, self.prompt_len = n_prompt, prompt_len
        self.n_dec, self.dec_len = n_dec, dec_len
        self.tp = n_prompt * prompt_len
        self.ts = n_dec * dec_len
        self.t = self.tp + self.ts

    def mod_row(self, i, tm):
        p_tiles = self.tp // tm
        per_seq = self.dec_len // tm
        return jnp.where(i < p_tiles, 0, 1 + (i - p_tiles) // per_seq)


def _vec_spec(row):
    return pl.BlockSpec((None, 1, D_MODEL), lambda i: (row, 0, 0))


def _mod_spec(lay, tm, col):
    return pl.BlockSpec((None, 1, D_MODEL), lambda i: (lay.mod_row(i, tm), 0, col))


def _mod_kernel(c_ref, w_ref, b_ref, o_ref):
    c = c_ref[...]
    s = (c * jax.nn.sigmoid(c)).astype(BF16)
    o_ref[...] = jnp.dot(s, w_ref[...].astype(BF16), preferred_element_type=F32) + b_ref[...]


def _modulation(cond, w_mod, b_mod):
    tn = 1024
    n = N_MOD * D_MODEL
    return pl.pallas_call(
        _mod_kernel,
        grid=(DEPTH, n // tn),
        in_specs=[pl.BlockSpec((MOD_ROWS, D_MODEL), lambda l, j: (0, 0)),
                  pl.BlockSpec((None, D_MODEL, tn), lambda l, j: (l, 0, j)),
                  pl.BlockSpec((None, 1, tn), lambda l, j: (l, 0, j))],
        out_specs=pl.BlockSpec((None, MOD_ROWS, tn), lambda l, j: (l, 0, j)),
        out_shape=jax.ShapeDtypeStruct((DEPTH, MOD_ROWS, n), F32),
        compiler_params=pltpu.CompilerParams(dimension_semantics=("arbitrary", "arbitrary"),
                                             vmem_limit_bytes=24 * MIB),
        name="modulation",
    )(cond, w_mod, b_mod.reshape(DEPTH, 1, n))


def _ffn_kernel(*refs, weight, p_tiles, split_in, split_out, sub):
    n_x = 2 if split_in else 1
    x_refs, o_refs = refs[:n_x], refs[n_x + 8:]
    sh_ref, sc_ref, gt_ref, g1_ref, g2_ref, wg_ref, wu_ref, wd_ref = refs[n_x:n_x + 8]
    is_prompt = pl.program_id(0) < p_tiles
    tm = x_refs[0].shape[0]
    n_sub = tm // sub
    xs, ab, ys, outs = {}, {}, {}, []

    def load_and_up(r):
        rows = slice(r * sub, (r + 1) * sub)
        if split_in:
            x = jnp.where(is_prompt, x_refs[0][rows, :], x_refs[1][rows, :])
        else:
            x = x_refs[0][rows, :]
        h = (_rms(x, g1_ref[...]) * (1.0 + sc_ref[...]) + sh_ref[...]).astype(BF16)
        xs[r] = x
        ab[r] = (jnp.dot(h, wg_ref[...], preferred_element_type=F32),
                 jnp.dot(h, wu_ref[...], preferred_element_type=F32))

    def gate_and_down(r):
        a, b = ab.pop(r)
        act = (a * jax.nn.sigmoid(a) * b).astype(BF16)
        ys[r] = jnp.dot(act, wd_ref[...], preferred_element_type=F32)

    def finish(r):
        outs.append(xs.pop(r) + weight * gt_ref[...] * _rms(ys.pop(r), g2_ref[...]))
        if not split_out:
            o_refs[0][r * sub:(r + 1) * sub, :] = outs[-1]

    for step in range(n_sub + 2):
        if step < n_sub:
            load_and_up(step)
        if 0 <= step - 1 < n_sub:
            gate_and_down(step - 1)
        if 0 <= step - 2 < n_sub:
            finish(step - 2)
    if split_out:
        out = jnp.concatenate(outs, axis=0)

        @pl.when(is_prompt)
        def _():
            o_refs[0][...] = out

        @pl.when(jnp.logical_not(is_prompt))
        def _():
            o_refs[1][...] = out


def _ffn(lay, xs, mod_l, norm_w, wg, wu, wd, l, s, slot, split_out=False):
    split_in = isinstance(xs, tuple)
    tm = 512 if (split_in or split_out) else 1024
    p_tiles = lay.tp // tm
    tile = pl.BlockSpec((tm, D_MODEL), lambda i: (i, 0))
    p_tile = pl.BlockSpec((tm, D_MODEL), lambda i: (jnp.minimum(i, p_tiles - 1), 0))
    s_tile = pl.BlockSpec((tm, D_MODEL), lambda i: (jnp.maximum(i - p_tiles, 0), 0))
    xs = xs if split_in else (xs,)
    if split_out:
        out_specs = [p_tile, s_tile]
        out_shape = [jax.ShapeDtypeStruct((lay.tp, D_MODEL), F32), jax.ShapeDtypeStruct((lay.ts, D_MODEL), F32)]
    else:
        out_specs, out_shape = tile, jax.ShapeDtypeStruct((lay.t, D_MODEL), F32)
    return pl.pallas_call(
        functools.partial(_ffn_kernel, weight=0.5, p_tiles=p_tiles, split_in=split_in, split_out=split_out,
                          sub=256),
        grid=(lay.t // tm,),
        in_specs=([p_tile, s_tile] if split_in else [tile]) + [
            _mod_spec(lay, tm, 3 * slot), _mod_spec(lay, tm, 3 * slot + 1), _mod_spec(lay, tm, 3 * slot + 2),
            _vec_spec(l * 6 + 2 * slot), _vec_spec(l * 6 + 2 * slot + 1),
            _resident((None, None, D_MODEL, D_FF), (l, s, 0, 0)),
            _resident((None, None, D_MODEL, D_FF), (l, s, 0, 0)),
            _resident((None, None, D_FF, D_MODEL), (l, s, 0, 0))],
        out_specs=out_specs,
        out_shape=out_shape,
        compiler_params=_params(52),
        name="swiglu",
    )(*xs, mod_l, mod_l, mod_l, norm_w, norm_w, wg, wu, wd)


def _cm_in_kernel(x_ref, sh_ref, sc_ref, g_ref, win_ref, slg_ref, slb_ref, sw_ref, sb_ref,
                  gout_ref, bout_ref, *, tm, sub):
    gc = SGU_CH // SGU_GROUPS
    zs = []
    for r in range(tm // sub):
        x = x_ref[r * sub:(r + 1) * sub, :]
        h = (_rms(x, g_ref[...]) * (1.0 + sc_ref[...]) + sh_ref[...]).astype(BF16)
        zs.append(jnp.dot(h, win_ref[...], preferred_element_type=F32))
    for r, z in enumerate(zs):
        a_val = z[:, :CONV_CH]
        a_gate = z[:, CONV_CH:2 * CONV_CH]
        u = jax.nn.gelu(z[:, 2 * CONV_CH:2 * CONV_CH + SGU_CH])
        v = _ln(jax.nn.gelu(z[:, 2 * CONV_CH + SGU_CH:]), slg_ref[...], slb_ref[...]).astype(BF16)
        gout_ref[r * sub:(r + 1) * sub, :] = a_val * jax.nn.sigmoid(a_gate)
        for n in range(sub // CHUNK):
            rows = slice(n * CHUNK, (n + 1) * CHUNK)
            out_rows = slice(r * sub + n * CHUNK, r * sub + (n + 1) * CHUNK)
            for gi in range(SGU_GROUPS):
                cols = slice(gi * gc, (gi + 1) * gc)
                vc = jnp.dot(sw_ref[gi], v[rows, cols], preferred_element_type=F32) + sb_ref[:, gi:gi + 1]
                bout_ref[out_rows, cols] = u[rows, cols] * vc


def _cm_in(lay, x, mod_l, norm_w, w_in, sln_g, sln_b, sgu_w, sgu_bt, l, j):
    tm = 512
    tile = pl.BlockSpec((tm, D_MODEL), lambda i: (i, 0))
    half = pl.BlockSpec((tm, CONV_CH), lambda i: (i, 0))
    n_in = 2 * CONV_CH + 2 * SGU_CH
    return pl.pallas_call(
        functools.partial(_cm_in_kernel, tm=tm, sub=256),
        grid=(lay.t // tm,),
        in_specs=[tile, _mod_spec(lay, tm, 3), _mod_spec(lay, tm, 4), _vec_spec(l * 6 + 2),
                  _resident((None, D_MODEL, n_in), (j, 0, 0)),
                  pl.BlockSpec((None, 1, SGU_CH), lambda i: (j, 0, 0)),
                  pl.BlockSpec((None, 1, SGU_CH), lambda i: (j, 0, 0)),
                  _resident((None, SGU_GROUPS, CHUNK, CHUNK), (j, 0, 0, 0)),
                  pl.BlockSpec((None, CHUNK, SGU_GROUPS), lambda i: (j, 0, 0))],
        out_specs=[half, half],
        out_shape=[jax.ShapeDtypeStruct((lay.t, CONV_CH), F32),
                   jax.ShapeDtypeStruct((lay.t, SGU_CH), F32)],
        compiler_params=_params(40),
        name="conv_gmlp_in",
    )(x, mod_l, mod_l, norm_w, w_in, sln_g, sln_b, sgu_w, sgu_bt)


def _cm_out_kernel(x_ref, gt_ref, g3_ref, gp_ref, gc_ref, gn_ref, bo_ref, cw_ref, cb_ref,
                   clg_ref, clb_ref, wout_ref, o_ref, gbuf, sbuf, abuf, *, tm, p_tiles, per_seq):
    i = pl.program_id(0)
    local = (i - p_tiles) % per_seq
    has_left = jnp.logical_and(i >= p_tiles, local != 0)
    has_right = jnp.logical_and(i >= p_tiles, local != per_seq - 1)
    gbuf[0:HALO, :] = jnp.where(has_left, gp_ref[...], 0.0)
    gbuf[HALO:HALO + tm, :] = gc_ref[...]
    gbuf[HALO + tm:, :] = jnp.where(has_right, gn_ref[...], 0.0)
    span = tm + 2 * HALO - SUBLANES
    for s in range(1, SUBLANES):
        sbuf[s - 1] = gbuf[s:s + span, :]
    rb = 32
    for r in range(tm // rb):
        acc = jnp.broadcast_to(cb_ref[...], (rb // SUBLANES, SUBLANES, CONV_CH))
        for k in range(CONV_WIDTH):
            off = HALO - CONV_HALF + k
            s, base = off % SUBLANES, r * rb + off - off % SUBLANES
            src = gbuf[base:base + rb, :] if s == 0 else sbuf[s - 1, base:base + rb, :]
            acc = acc + cw_ref[k][None] * src.reshape(rb // SUBLANES, SUBLANES, CONV_CH)
        a = _ln(acc.reshape(rb, CONV_CH), clg_ref[...], clb_ref[...])
        abuf[r * rb:(r + 1) * rb, :] = (a * jax.nn.sigmoid(a)).astype(BF16)
    y = jnp.dot(abuf[...], wout_ref[:CONV_CH, :], preferred_element_type=F32)
    y = y + jnp.dot(bo_ref[...].astype(BF16), wout_ref[CONV_CH:, :], preferred_element_type=F32)
    o_ref[...] = x_ref[...] + gt_ref[...] * _rms(y, g3_ref[...])


def _cm_out(lay, x, mod_l, norm_w, g, b_out, conv_w, conv_b, cln_g, cln_b, w_out, l, j):
    tm = 256
    assert lay.prompt_len == tm and lay.dec_len % tm == 0
    hb = tm // HALO
    last = lay.t // HALO - 1
    tile = pl.BlockSpec((tm, D_MODEL), lambda i: (i, 0))
    half = pl.BlockSpec((tm, CONV_CH), lambda i: (i, 0))
    vec = pl.BlockSpec((None, 1, CONV_CH), lambda i: (j, 0, 0))
    return pl.pallas_call(
        functools.partial(_cm_out_kernel, tm=tm, p_tiles=lay.tp // tm, per_seq=lay.dec_len // tm),
        grid=(lay.t // tm,),
        in_specs=[tile, _mod_spec(lay, tm, 5), _vec_spec(l * 6 + 3),
                  pl.BlockSpec((HALO, CONV_CH), lambda i: (jnp.maximum(i * hb - 1, 0), 0)),
                  half,
                  pl.BlockSpec((HALO, CONV_CH), lambda i: (jnp.minimum((i + 1) * hb, last), 0)),
                  half,
                  pl.BlockSpec((None, CONV_WIDTH, SUBLANES, CONV_CH), lambda i: (j, 0, 0, 0)),
                  vec, vec, vec,
                  _resident((None, CONV_CH + SGU_CH, D_MODEL), (j, 0, 0))],
        out_specs=tile,
        out_shape=jax.ShapeDtypeStruct((lay.t, D_MODEL), F32),
        scratch_shapes=[pltpu.VMEM((tm + 2 * HALO, CONV_CH), F32),
                        pltpu.VMEM((SUBLANES - 1, tm + 2 * HALO - SUBLANES, CONV_CH), F32),
                        pltpu.VMEM((tm, CONV_CH), BF16)],
        compiler_params=_params(32),
        name="conv_gmlp_out",
    )(x, mod_l, norm_w, g, g, g, b_out, conv_w, conv_b, cln_g, cln_b, w_out)


def _rope_tables(dec_len, pad_rows):
    pos = np.arange(dec_len)
    r = (pos // GRID_W).astype(np.float64)
    c = (pos % GRID_W).astype(np.float64)
    inv = np.power(ROPE_BASE, -np.arange(0, ROPE_HALF, 2, dtype=np.float64) / ROPE_HALF)
    ang_r = r[:, None] * inv[None, :]
    ang_c = c[:, None] * inv[None, :]
    cos = np.concatenate([np.cos(ang_r)] * 2 + [np.cos(ang_c)] * 2, axis=-1)
    sin = np.concatenate([-np.sin(ang_r), np.sin(ang_r), -np.sin(ang_c), np.sin(ang_c)], axis=-1)
    cos = np.concatenate([np.ones((pad_rows, HEAD_DIM)), cos], axis=0)
    sin = np.concatenate([np.zeros((pad_rows, HEAD_DIM)), sin], axis=0)
    return (jnp.asarray(np.tile(cos, (1, 2)), F32), jnp.asarray(np.tile(sin, (1, 2)), F32))


def _rope(z, cos, sin, first_half):
    partner = jnp.where(first_half, pltpu.roll(z, 128 - ROPE_HALF // 2, 1), pltpu.roll(z, ROPE_HALF // 2, 1))
    return z * cos + partner * sin


def _qkv_kernel(x_ref, sh_ref, sc_ref, g_ref, w_ref, cos_ref, sin_ref, q_ref, k_ref, v_ref, kb_ref, vb_ref,
                *, tm, sub):
    lane = lax.broadcasted_iota(jnp.int32, (sub, 128), 1)
    first_half = (lane % ROPE_HALF) < (ROPE_HALF // 2)
    zs = []
    for r in range(tm // sub):
        x = x_ref[r * sub:(r + 1) * sub, :]
        h = (_rms(x, g_ref[...]) * (1.0 + sc_ref[...]) + sh_ref[...]).astype(BF16)
        zs.append(jnp.dot(h, w_ref[...], preferred_element_type=F32))
    for r, z in enumerate(zs):
        rows = slice(r * sub, (r + 1) * sub)
        cos = cos_ref[rows, :]
        sin = sin_ref[rows, :]
        for c in range(Q_DIM // 128):
            zc = _rope(z[:, c * 128:(c + 1) * 128], cos, sin, first_half)
            q_ref[rows, c * 128:(c + 1) * 128] = (zc * (ATTN_SCALE * LOG2E)).astype(BF16)
        for c in range(KV_DIM // 128):
            lo = Q_DIM + c * 128
            kc = _rope(z[:, lo:lo + 128], cos, sin, first_half)
            k_ref[rows, c * 128:(c + 1) * 128] = kc
            kb_ref[rows, c * 128:(c + 1) * 128] = kc.astype(BF16)
        v = z[:, Q_DIM + KV_DIM:]
        v_ref[rows, :] = v
        vb_ref[rows, :] = v.astype(BF16)


def _qkv(lay, x, mod_l, norm_w, w_qkv, cos_t, sin_t, l, j, tm):
    p_tiles = lay.tp // tm
    per_seq = lay.dec_len // tm
    tile = pl.BlockSpec((tm, D_MODEL), lambda i: (i, 0))
    tab = pl.BlockSpec((tm, 128), lambda i: (jnp.where(i < p_tiles, 0, 1 + (i - p_tiles) % per_seq), 0))
    kv = pl.BlockSpec((tm, KV_DIM), lambda i: (i, 0))
    return pl.pallas_call(
        functools.partial(_qkv_kernel, tm=tm, sub=256),
        grid=(lay.t // tm,),
        in_specs=[tile, _mod_spec(lay, tm, 3), _mod_spec(lay, tm, 4), _vec_spec(l * 6 + 2),
                  _resident((None, D_MODEL, Q_DIM + 2 * KV_DIM), (j, 0, 0)), tab, tab],
        out_specs=[pl.BlockSpec((tm, Q_DIM), lambda i: (i, 0)), kv, kv, kv, kv],
        out_shape=[jax.ShapeDtypeStruct((lay.t, Q_DIM), BF16),
                   jax.ShapeDtypeStruct((lay.t, KV_DIM), F32),
                   jax.ShapeDtypeStruct((lay.t, KV_DIM), F32),
                   jax.ShapeDtypeStruct((lay.t, KV_DIM), BF16),
                   jax.ShapeDtypeStruct((lay.t, KV_DIM), BF16)],
        compiler_params=_params(40),
        name="qkv_rope",
    )(x, mod_l, mod_l, norm_w, w_qkv, cos_t, sin_t)


def _attend(q_ref, sink_ref, o_scr, s_scr, keys, values, masks, j):
    rowblk = lax.broadcasted_iota(jnp.int32, (GROUP * BLOCK, 1), 0) // BLOCK
    low_lanes = lax.broadcasted_iota(jnp.int32, (BLOCK, 2 * HEAD_DIM), 1) < HEAD_DIM

    def score_pass(kvh):
        lanes = slice(kvh * HEAD_DIM, (kvh + 1) * HEAD_DIM)
        q = jnp.concatenate(
            [q_ref[:, (kvh * GROUP + g) * HEAD_DIM:(kvh * GROUP + g + 1) * HEAD_DIM] for g in range(GROUP)],
            axis=0)
        sink = jnp.zeros((GROUP * BLOCK, 1), F32)
        for g in range(GROUP):
            sink = jnp.where(rowblk == g, sink_ref[j, kvh * GROUP + g] * LOG2E, sink)
        cols, lo = [], 0
        for k_ref, mask in zip(keys, masks):
            s = lax.dot_general(q, k_ref[:, lanes], (((1,), (1,)), ((), ())), preferred_element_type=F32)
            if mask is not None:
                s = jnp.where(mask, s, NEG_INF)
            s_scr[kvh, :, lo:lo + s.shape[1]] = s
            lo += s.shape[1]
            cols += [s[:, c * BLOCK:(c + 1) * BLOCK] for c in range(s.shape[1] // BLOCK)]
        m = jnp.maximum(jnp.max(functools.reduce(jnp.maximum, cols), axis=-1, keepdims=True), sink)
        return sink, m

    def value_pass(kvh, sink, m):
        pair = slice((kvh // 2) * 2 * HEAD_DIM, (kvh // 2 + 1) * 2 * HEAD_DIM)
        psum = jnp.zeros((GROUP * BLOCK, BLOCK), F32)
        o2 = jnp.zeros((GROUP * BLOCK, 2 * HEAD_DIM), F32)
        lo = 0
        for v_ref in values:
            rows = v_ref.shape[0]
            p = jnp.exp2(s_scr[kvh, :, lo:lo + rows] - m)
            lo += rows
            for c in range(rows // BLOCK):
                psum = psum + p[:, c * BLOCK:(c + 1) * BLOCK]
            o2 = o2 + jnp.dot(p.astype(BF16), v_ref[:, pair], preferred_element_type=F32)
        denom = jnp.sum(psum, axis=-1, keepdims=True) + jnp.exp2(sink - m)
        o2 = o2 * (1.0 / denom)
        for half in range(GROUP // 2):
            a = o2[(2 * half) * BLOCK:(2 * half + 1) * BLOCK]
            b = o2[(2 * half + 1) * BLOCK:(2 * half + 2) * BLOCK]
            if kvh % 2 == 0:
                b = pltpu.roll(b, HEAD_DIM, 1)
            else:
                a = pltpu.roll(a, HEAD_DIM, 1)
            c = kvh * (GROUP // 2) + half
            o_scr[:, c * 2 * HEAD_DIM:(c + 1) * 2 * HEAD_DIM] = jnp.where(low_lanes, a, b).astype(BF16)

    parked = {}
    for step in range(N_KV_HEADS + 1):
        if step < N_KV_HEADS:
            parked[step] = score_pass(step)
        if step >= 1:
            value_pass(step - 1, *parked.pop(step - 1))


def _attn_kernel(sink_ref, x_ref, gt_ref, g3_ref, q_ref, kp_ref, kc_ref, kn_ref, vp_ref, vc_ref, vn_ref,
                 ck_ref, cv_ref, wo_ref, o_ref, o_scr, s_scr, *, p_blocks, per_seq, j):
    i = pl.program_id(0)
    is_prompt = i < p_blocks

    @pl.when(is_prompt)
    def _():
        _attend(q_ref, sink_ref, o_scr, s_scr, [kp_ref, kc_ref], [vp_ref, vc_ref], [None, None], j)

    @pl.when(jnp.logical_not(is_prompt))
    def _():
        local = (i - p_blocks) % per_seq
        r = lax.broadcasted_iota(jnp.int32, (GROUP * BLOCK, BLOCK), 0) % BLOCK
        c = lax.broadcasted_iota(jnp.int32, (GROUP * BLOCK, BLOCK), 1)
        prev_mask = jnp.logical_and(c >= r, local != 0)
        next_mask = jnp.logical_and(c <= r, local != per_seq - 1)
        _attend(q_ref, sink_ref, o_scr, s_scr, [kp_ref, kc_ref, kn_ref, ck_ref],
                [vp_ref, vc_ref, vn_ref, cv_ref], [prev_mask, None, next_mask, None], j)

    y = jnp.dot(o_scr[...], wo_ref[...], preferred_element_type=F32)
    o_ref[...] = x_ref[...] + gt_ref[...] * _rms(y, g3_ref[...])


def _attention(lay, x, mod_l, norm_w, q, k, v, cache_k, cache_v, sink, w_o, l, j):
    tm = BLOCK
    assert lay.prompt_len == 2 * BLOCK and lay.dec_len % BLOCK == 0
    p_blocks = lay.tp // tm
    per_seq = lay.dec_len // tm

    def prev_idx(i):
        local = (i - p_blocks) % per_seq
        return jnp.where(i < p_blocks, i ^ 1, jnp.where(local == 0, i, i - 1))

    def next_idx(i):
        local = (i - p_blocks) % per_seq
        return jnp.where(i < p_blocks, i, jnp.where(local == per_seq - 1, i, i + 1))

    def cache_idx(i):
        return jnp.where(i < p_blocks, 0, (i - p_blocks) // per_seq)

    tile = pl.BlockSpec((tm, D_MODEL), lambda i: (i, 0))
    kv_prev = pl.BlockSpec((tm, KV_DIM), lambda i: (prev_idx(i), 0))
    kv_cur = pl.BlockSpec((tm, KV_DIM), lambda i: (i, 0))
    kv_next = pl.BlockSpec((tm, KV_DIM), lambda i: (next_idx(i), 0))
    past = cache_k.shape[2]
    ctx = pl.BlockSpec((None, None, past, KV_DIM), lambda i: (cache_idx(i), j, 0, 0))
    return pl.pallas_call(
        functools.partial(_attn_kernel, p_blocks=p_blocks, per_seq=per_seq, j=j),
        grid=(lay.t // tm,),
        in_specs=[pl.BlockSpec(memory_space=pltpu.SMEM),
                  tile, _mod_spec(lay, tm, 5), _vec_spec(l * 6 + 3),
                  pl.BlockSpec((tm, Q_DIM), lambda i: (i, 0)),
                  kv_prev, kv_cur, kv_next, kv_prev, kv_cur, kv_next, ctx, ctx,
                  _resident((None, Q_DIM, D_MODEL), (j, 0, 0))],
        out_specs=tile,
        out_shape=jax.ShapeDtypeStruct((lay.t, D_MODEL), F32),
        scratch_shapes=[pltpu.VMEM((tm, Q_DIM), BF16),
                        pltpu.VMEM((N_KV_HEADS, GROUP * BLOCK, 3 * BLOCK + past), F32)],
        compiler_params=_params(32),
        name="window_attention",
    )(sink, x, mod_l, norm_w, q, k, k, k, v, v, v, cache_k, cache_v, w_o)


def kernel(x_prompt, x_sample, cache_k, cache_v, c, c_ctx, w_mod, b_mod, norm_w, ffn_w_gate, ffn_w_up,
           ffn_w_down, cm_w_in, cm_conv_w, cm_conv_b, cm_conv_ln_g, cm_conv_ln_b, cm_sgu_ln_g, cm_sgu_ln_b,
           cm_sgu_w, cm_sgu_b, cm_w_out, attn_w_qkv, attn_w_o, attn_sink):
    n_prompt, prompt_len, _ = x_prompt.shape
    n_dec, dec_len, _ = x_sample.shape
    lay = _Layout(n_prompt, prompt_len, n_dec, dec_len)
    n_attn = attn_w_qkv.shape[0]
    n_cm = cm_w_in.shape[0]
    past = cache_k.shape[2]

    x = (x_prompt.reshape(lay.tp, D_MODEL), x_sample.reshape(lay.ts, D_MODEL))
    cond =jnp.concatenate([c_ctx[None, :], c, jnp.zeros((MOD_ROWS - 1 - n_dec, D_MODEL), F32)], axis=0)
    mod = _modulation(cond, w_mod, b_mod).reshape(DEPTH, MOD_ROWS, 1, N_MOD * D_MODEL)
    norm_rows = norm_w.reshape(DEPTH * 6, 1, D_MODEL)

    wg = ffn_w_gate.astype(BF16)
    wu = ffn_w_up.astype(BF16)
    wd = ffn_w_down.astype(BF16)
    w_in = cm_w_in.astype(BF16)
    w_out = cm_w_out.astype(BF16)
    sgu_w = cm_sgu_w.astype(BF16)
    sgu_bt = jnp.swapaxes(cm_sgu_b, 1, 2)
    w_qkv = attn_w_qkv.astype(BF16)
    w_o = attn_w_o.astype(BF16)
    vec_cm = lambda a: a.reshape(n_cm, 1, -1)
    conv_w = jnp.broadcast_to(cm_conv_w[:, :, None, :], (n_cm, CONV_WIDTH, SUBLANES, CONV_CH))
    ck = cache_k.reshape(n_dec, n_attn, past, KV_DIM).astype(BF16)
    cv = cache_v.reshape(n_dec, n_attn, past, KV_DIM).astype(BF16)
    qkv_tm = 512
    cos_t, sin_t = _rope_tables(dec_len, qkv_tm)

    new_k, new_v = [], []
    for l in range(DEPTH):
        mod_l = mod[l]
        j = l // 2
        x = _ffn(lay, x, mod_l, norm_rows, wg, wu, wd, l, 0, 0)
        if l % 2 == 0:
            g, b_out = _cm_in(lay, x, mod_l, norm_rows, w_in, vec_cm(cm_sgu_ln_g), vec_cm(cm_sgu_ln_b),
                              sgu_w, sgu_bt, l, j)
            x = _cm_out(lay, x, mod_l, norm_rows, g, b_out, conv_w, vec_cm(cm_conv_b),
                        vec_cm(cm_conv_ln_g), vec_cm(cm_conv_ln_b), w_out, l, j)
        else:
            q, k, v, kb, vb = _qkv(lay, x, mod_l, norm_rows, w_qkv, cos_t, sin_t, l, j, qkv_tm)
            new_k.append(k[:lay.tp].reshape(n_prompt, prompt_len, N_KV_HEADS, HEAD_DIM))
            new_v.append(v[:lay.tp].reshape(n_prompt, prompt_len, N_KV_HEADS, HEAD_DIM))
            x = _attention(lay, x, mod_l, norm_rows, q, kb, vb, ck, cv, attn_sink, w_o, l, j)
        x = _ffn(lay, x, mod_l, norm_rows, wg, wu, wd, l, 1, 2, split_out=(l == DEPTH - 1))
    y_prompt, y_sample = x
    return (y_prompt.reshape(x_prompt.shape), y_sample.reshape(x_sample.shape),
            jnp.stack(new_k, axis=1), jnp.stack(new_v, axis=1))
```

```python
import functools

import numpy as np
import jax
import jax.numpy as jnp
from jax import lax
from jax.experimental import pallas as pl
from jax.experimental.pallas import tpu as pltpu

F32 = jnp.float32
BF16 = jnp.bfloat16

D_MODEL = 1024
DEPTH = 4
GRID_W = 64
N_MOD = 9
D_FF = 2816
CONV_CH = 512
CONV_WIDTH = 31
CONV_HALF = CONV_WIDTH // 2
SGU_CH = 512
SGU_GROUPS = 4
CHUNK = 128
HEAD_DIM = 64
N_HEADS = 16
N_KV_HEADS = 4
GROUP = N_HEADS // N_KV_HEADS
Q_DIM = N_HEADS * HEAD_DIM
KV_DIM = N_KV_HEADS * HEAD_DIM
BLOCK = 128
ROPE_HALF = HEAD_DIM // 2
ROPE_BASE = 10000.0
EPS = 1e-6
NEG_INF = -1e30
ATTN_SCALE = HEAD_DIM ** -0.5
LOG2E = float(np.log2(np.e))

SUBLANES = 8
MOD_ROWS = 8
HALO = 16
MIB = 1024 * 1024


def _rms(x, g):
    return x * lax.rsqrt(jnp.mean(x * x, axis=-1, keepdims=True) + EPS) * g


def _ln(x, g, b):
    mu = jnp.mean(x, axis=-1, keepdims=True)
    xc = x - mu
    var = jnp.mean(xc * xc, axis=-1, keepdims=True)
    return xc * lax.rsqrt(var + EPS) * g + b


def _resident(block_shape, index):
    return pl.BlockSpec(block_shape, lambda i: index, pipeline_mode=pl.Buffered(1))


def _params(vmem_mib):
    return pltpu.CompilerParams(dimension_semantics=("arbitrary",),
                                vmem_limit_bytes=vmem_mib * MIB)


class _Layout:
    def __init__(self, n_prompt, prompt_len, n_dec, dec_len):
        self.n_prompt, self.prompt_len = n_prompt, prompt_len
        self.n_dec, self.dec_len = n_dec, dec_len
        self.tp = n_prompt * prompt_len
        self.ts = n_dec * dec_len
        self.t = self.tp + self.ts

    def mod_row(self, i, tm):
        p_tiles = self.tp // tm
        per_seq = self.dec_len // tm
        return jnp.where(i < p_tiles, 0, 1 + (i - p_tiles) // per_seq)


def _vec_spec(row):
    return pl.BlockSpec((None, 1, D_MODEL), lambda i: (row, 0, 0))


def _mod_spec(lay, tm, col):
    return pl.BlockSpec((None, 1, D_MODEL), lambda i: (lay.mod_row(i, tm), 0, col))


def _mod_kernel(c_ref, w_ref, b_ref, o_ref):
    c = c_ref[...]
    s = (c * jax.nn.sigmoid(c)).astype(BF16)
    o_ref[...] = jnp.dot(s, w_ref[...].astype(BF16), preferred_element_type=F32) + b_ref[...]


def _modulation(cond, w_mod, b_mod):
    tn = 1024
    n = N_MOD * D_MODEL
    return pl.pallas_call(
        _mod_kernel,
        grid=(DEPTH, n // tn),
        in_specs=[pl.BlockSpec((MOD_ROWS, D_MODEL), lambda l, j: (0, 0)),
                  pl.BlockSpec((None, D_MODEL, tn), lambda l, j: (l, 0, j)),
                  pl.BlockSpec((None, 1, tn), lambda l, j: (l, 0, j))],
        out_specs=pl.BlockSpec((None, MOD_ROWS, tn), lambda l, j: (l, 0, j)),
        out_shape=jax.ShapeDtypeStruct((DEPTH, MOD_ROWS, n), F32),
        compiler_params=pltpu.CompilerParams(dimension_semantics=("arbitrary", "arbitrary"),
                                             vmem_limit_bytes=24 * MIB),
        name="modulation",
    )(cond, w_mod, b_mod.reshape(DEPTH, 1, n))


def _ffn_kernel(*refs, weight, p_tiles, split_in, split_out, sub):
    n_x = 2 if split_in else 1
    x_refs, o_refs = refs[:n_x], refs[n_x + 8:]
    sh_ref, sc_ref, gt_ref, g1_ref, g2_ref, wg_ref, wu_ref, wd_ref = refs[n_x:n_x + 8]
    is_prompt = pl.program_id(0) < p_tiles
    tm = x_refs[0].shape[0]
    n_sub = tm // sub
    xs, ab, ys, outs = {}, {}, {}, []

    def load_and_up(r):
        rows = slice(r * sub, (r + 1) * sub)
        if split_in:
            x = jnp.where(is_prompt, x_refs[0][rows, :], x_refs[1][rows, :])
        else:
            x = x_refs[0][rows, :]
        h = (_rms(x, g1_ref[...]) * (1.0 + sc_ref[...]) + sh_ref[...]).astype(BF16)
        xs[r] = x
        ab[r] = (jnp.dot(h, wg_ref[...], preferred_element_type=F32),
                 jnp.dot(h, wu_ref[...], preferred_element_type=F32))

    def gate_and_down(r):
        a, b = ab.pop(r)
        act = (a * jax.nn.sigmoid(a) * b).astype(BF16)
        ys[r] = jnp.dot(act, wd_ref[...], preferred_element_type=F32)

    def finish(r):
        outs.append(xs.pop(r) + weight * gt_ref[...] * _rms(ys.pop(r), g2_ref[...]))
        if not split_out:
            o_refs[0][r * sub:(r + 1) * sub, :] = outs[-1]

    for step in range(n_sub + 2):
        if step < n_sub:
            load_and_up(step)
        if 0 <= step - 1 < n_sub:
            gate_and_down(step - 1)
        if 0 <= step - 2 < n_sub:
            finish(step - 2)
    if split_out:
        out = jnp.concatenate(outs, axis=0)

        @pl.when(is_prompt)
        def _():
            o_refs[0][...] = out

        @pl.when(jnp.logical_not(is_prompt))
        def _():
            o_refs[1][...] = out


def _ffn(lay, xs, mod_l, norm_w, wg, wu, wd, l, s, slot, split_out=False):
    split_in = isinstance(xs, tuple)
    tm = 512 if (split_in or split_out) else 1024
    p_tiles = lay.tp // tm
    tile = pl.BlockSpec((tm, D_MODEL), lambda i: (i, 0))
    p_tile = pl.BlockSpec((tm, D_MODEL), lambda i: (jnp.minimum(i, p_tiles - 1), 0))
    s_tile = pl.BlockSpec((tm, D_MODEL), lambda i: (jnp.maximum(i - p_tiles, 0), 0))
    xs = xs if split_in else (xs,)
    if split_out:
        out_specs = [p_tile, s_tile]
        out_shape = [jax.ShapeDtypeStruct((lay.tp, D_MODEL), F32), jax.ShapeDtypeStruct((lay.ts, D_MODEL), F32)]
    else:
        out_specs, out_shape = tile, jax.ShapeDtypeStruct((lay.t, D_MODEL), F32)
    return pl.pallas_call(
        functools.partial(_ffn_kernel, weight=0.5, p_tiles=p_tiles, split_in=split_in, split_out=split_out,
                          sub=256),
        grid=(lay.t // tm,),
        in_specs=([p_tile, s_tile] if split_in else [tile]) + [
            _mod_spec(lay, tm, 3 * slot), _mod_spec(lay, tm, 3 * slot + 1), _mod_spec(lay, tm, 3 * slot + 2),
            _vec_spec(l * 6 + 2 * slot), _vec_spec(l * 6 + 2 * slot + 1),
            _resident((None, None, D_MODEL, D_FF), (l, s, 0, 0)),
            _resident((None, None, D_MODEL, D_FF), (l, s, 0, 0)),
            _resident((None, None, D_FF, D_MODEL), (l, s, 0, 0))],
        out_specs=out_specs,
        out_shape=out_shape,
        compiler_params=_params(52),
        name="swiglu",
    )(*xs, mod_l, mod_l, mod_l, norm_w, norm_w, wg, wu, wd)


def _cm_in_kernel(x_ref, sh_ref, sc_ref, g_ref, win_ref, slg_ref, slb_ref, sw_ref, sb_ref,
                  gout_ref, bout_ref, *, tm, sub):
    gc = SGU_CH // SGU_GROUPS
    n_sub = tm // sub
    zs = {}

    def project(r):
        x = x_ref[r * sub:(r + 1) * sub, :]
        h = (_rms(x, g_ref[...]) * (1.0 + sc_ref[...]) + sh_ref[...]).astype(BF16)
        zs[r] = jnp.dot(h, win_ref[...], preferred_element_type=F32)

    project(0)
    for r in range(n_sub):
        if r + 1 < n_sub:
            project(r + 1)
        z = zs.pop(r)
        a_val = z[:, :CONV_CH]
        a_gate = z[:, CONV_CH:2 * CONV_CH]
        u = jax.nn.gelu(z[:, 2 * CONV_CH:2 * CONV_CH + SGU_CH])
        v = _ln(jax.nn.gelu(z[:, 2 * CONV_CH + SGU_CH:]), slg_ref[...], slb_ref[...]).astype(BF16)
        gout_ref[r * sub:(r + 1) * sub, :] = a_val * jax.nn.sigmoid(a_gate)
        for n in range(sub // CHUNK):
            rows = slice(n * CHUNK, (n + 1) * CHUNK)
            out_rows = slice(r * sub + n * CHUNK, r * sub + (n + 1) * CHUNK)
            for gi in range(SGU_GROUPS):
                cols = slice(gi * gc, (gi + 1) * gc)
                vc = jnp.dot(sw_ref[gi], v[rows, cols], preferred_element_type=F32) + sb_ref[:, gi:gi + 1]
                bout_ref[out_rows, cols] = u[rows, cols] * vc


def _cm_in(lay, x, mod_l, norm_w, w_in, sln_g, sln_b, sgu_w, sgu_bt, l, j):
    tm = 1024
    tile = pl.BlockSpec((tm, D_MODEL), lambda i: (i, 0))
    half = pl.BlockSpec((tm, CONV_CH), lambda i: (i, 0))
    n_in = 2 * CONV_CH + 2 * SGU_CH
    return pl.pallas_call(
        functools.partial(_cm_in_kernel, tm=tm, sub=256),
        grid=(lay.t // tm,),
        in_specs=[tile, _mod_spec(lay, tm, 3), _mod_spec(lay, tm, 4), _vec_spec(l * 6 + 2),
                  _resident((None, D_MODEL, n_in), (j, 0, 0)),
                  pl.BlockSpec((None, 1, SGU_CH), lambda i: (j, 0, 0)),
                  pl.BlockSpec((None, 1, SGU_CH), lambda i: (j, 0, 0)),
                  _resident((None, SGU_GROUPS, CHUNK, CHUNK), (j, 0, 0, 0)),
                  pl.BlockSpec((None, CHUNK, SGU_GROUPS), lambda i: (j, 0, 0))],
        out_specs=[half, half],
        out_shape=[jax.ShapeDtypeStruct((lay.t, CONV_CH), F32),
                   jax.ShapeDtypeStruct((lay.t, SGU_CH), F32)],
        compiler_params=_params(40),
        name="conv_gmlp_in",
    )(x, mod_l, mod_l, norm_w, w_in, sln_g, sln_b, sgu_w, sgu_bt)


def _cm_out_kernel(x_ref, gt_ref, g3_ref, gp_ref, gc_ref, gn_ref, bo_ref, cw_ref, cb_ref,
                   clg_ref, clb_ref, wout_ref, o_ref, gbuf, sbuf, abuf, *, tm, p_tiles, per_seq):
    i = pl.program_id(0)
    local = (i - p_tiles) % per_seq
    has_left = jnp.logical_and(i >= p_tiles, local != 0)
    has_right = jnp.logical_and(i >= p_tiles, local != per_seq - 1)
    gbuf[0:HALO, :] = jnp.where(has_left, gp_ref[...], 0.0)
    gbuf[HALO:HALO + tm, :] = gc_ref[...]
    gbuf[HALO + tm:, :] = jnp.where(has_right, gn_ref[...], 0.0)
    span = tm + 2 * HALO - SUBLANES
    for s in range(1, SUBLANES):
        sbuf[s - 1] = gbuf[s:s + span, :]
    rb = 32
    for r in range(tm // rb):
        acc = jnp.broadcast_to(cb_ref[...], (rb // SUBLANES, SUBLANES, CONV_CH))
        for k in range(CONV_WIDTH):
            off = HALO - CONV_HALF + k
            s, base = off % SUBLANES, r * rb + off - off % SUBLANES
            src = gbuf[base:base + rb, :] if s == 0 else sbuf[s - 1, base:base + rb, :]
            acc = acc + cw_ref[k][None] * src.reshape(rb // SUBLANES, SUBLANES, CONV_CH)
        a = _ln(acc.reshape(rb, CONV_CH), clg_ref[...], clb_ref[...])
        abuf[r * rb:(r + 1) * rb, :] = (a * jax.nn.sigmoid(a)).astype(BF16)
    y = jnp.dot(abuf[...], wout_ref[:CONV_CH, :], preferred_element_type=F32)
    y = y + jnp.dot(bo_ref[...].astype(BF16), wout_ref[CONV_CH:, :], preferred_element_type=F32)
    o_ref[...] = x_ref[...] + gt_ref[...] * _rms(y, g3_ref[...])


def _cm_out(lay, x, mod_l, norm_w, g, b_out, conv_w, conv_b, cln_g, cln_b, w_out, l, j):
    tm = 256
    assert lay.prompt_len == tm and lay.dec_len % tm == 0
    hb = tm // HALO
    last = lay.t // HALO - 1
    tile = pl.BlockSpec((tm, D_MODEL), lambda i: (i, 0))
    half = pl.BlockSpec((tm, CONV_CH), lambda i: (i, 0))
    vec = pl.BlockSpec((None, 1, CONV_CH), lambda i: (j, 0, 0))
    return pl.pallas_call(
        functools.partial(_cm_out_kernel, tm=tm, p_tiles=lay.tp // tm, per_seq=lay.dec_len // tm),
        grid=(lay.t // tm,),
        in_specs=[tile, _mod_spec(lay, tm, 5), _vec_spec(l * 6 + 3),
                  pl.BlockSpec((HALO, CONV_CH), lambda i: (jnp.maximum(i * hb - 1, 0), 0)),
                  half,
                  pl.BlockSpec((HALO, CONV_CH), lambda i: (jnp.minimum((i + 1) * hb, last), 0)),
                  half,
                  pl.BlockSpec((None, CONV_WIDTH, SUBLANES, CONV_CH), lambda i: (j, 0, 0, 0)),
                  vec, vec, vec,
                  _resident((None, CONV_CH + SGU_CH, D_MODEL), (j, 0, 0))],
        out_specs=tile,
        out_shape=jax.ShapeDtypeStruct((lay.t, D_MODEL), F32),
        scratch_shapes=[pltpu.VMEM((tm + 2 * HALO, CONV_CH), F32),
                        pltpu.VMEM((SUBLANES - 1, tm + 2 * HALO - SUBLANES, CONV_CH), F32),
                        pltpu.VMEM((tm, CONV_CH), BF16)],
        compiler_params=_params(32),
        name="conv_gmlp_out",
    )(x, mod_l, norm_w, g, g, g, b_out, conv_w, conv_b, cln_g, cln_b, w_out)


def _rope_tables(dec_len, pad_rows):
    pos = np.arange(dec_len)
    r = (pos // GRID_W).astype(np.float64)
    c = (pos % GRID_W).astype(np.float64)
    inv = np.power(ROPE_BASE, -np.arange(0, ROPE_HALF, 2, dtype=np.float64) / ROPE_HALF)
    ang_r = r[:, None] * inv[None, :]
    ang_c = c[:, None] * inv[None, :]
    cos = np.concatenate([np.cos(ang_r)] * 2 + [np.cos(ang_c)] * 2, axis=-1)
    sin = np.concatenate([-np.sin(ang_r), np.sin(ang_r), -np.sin(ang_c), np.sin(ang_c)], axis=-1)
    cos = np.concatenate([np.ones((pad_rows, HEAD_DIM)), cos], axis=0)
    sin = np.concatenate([np.zeros((pad_rows, HEAD_DIM)), sin], axis=0)
    return (jnp.asarray(np.tile(cos, (1, 2)), F32), jnp.asarray(np.tile(sin, (1, 2)), F32))


def _rope(z, cos, sin, first_half):
    partner = jnp.where(first_half, pltpu.roll(z, 128 - ROPE_HALF // 2, 1), pltpu.roll(z, ROPE_HALF // 2, 1))
    return z * cos + partner * sin


def _qkv_kernel(x_ref, sh_ref, sc_ref, g_ref, w_ref, cos_ref, sin_ref, q_ref, k_ref, v_ref, kb_ref, vb_ref,
                *, tm, sub, p_tiles):
    lane = lax.broadcasted_iota(jnp.int32, (sub, 128), 1)
    first_half = (lane % ROPE_HALF) < (ROPE_HALF // 2)
    n_sub = tm // sub
    zs, ks, vs = {}, [], []

    def project(r):
        x = x_ref[r * sub:(r + 1) * sub, :]
        h = (_rms(x, g_ref[...]) * (1.0 + sc_ref[...]) + sh_ref[...]).astype(BF16)
        zs[r] = jnp.dot(h, w_ref[...], preferred_element_type=F32)

    project(0)
    for r in range(n_sub):
        if r + 1 < n_sub:
            project(r + 1)
        z = zs.pop(r)
        rows = slice(r * sub, (r + 1) * sub)
        cos = cos_ref[rows, :]
        sin = sin_ref[rows, :]
        for c in range(Q_DIM // 128):
            zc = _rope(z[:, c * 128:(c + 1) * 128], cos, sin, first_half)
            q_ref[rows, c * 128:(c + 1) * 128] = (zc * (ATTN_SCALE * LOG2E)).astype(BF16)
        k = jnp.concatenate([_rope(z[:, Q_DIM + c * 128:Q_DIM + (c + 1) * 128], cos, sin, first_half)
                             for c in range(KV_DIM // 128)], axis=1)
        v = z[:, Q_DIM + KV_DIM:]
        kb_ref[rows, :] = k.astype(BF16)
        vb_ref[rows, :] = v.astype(BF16)
        ks.append(k)
        vs.append(v)

    @pl.when(pl.program_id(0) < p_tiles)
    def _():
        k_ref[...] = jnp.concatenate(ks, axis=0)
        v_ref[...] = jnp.concatenate(vs, axis=0)


def _qkv(lay, x, mod_l, norm_w, w_qkv, cos_t, sin_t, l, j, tm):
    p_tiles = lay.tp // tm
    per_seq = lay.dec_len // tm
    tile = pl.BlockSpec((tm, D_MODEL), lambda i: (i, 0))
    tab = pl.BlockSpec((tm, 128), lambda i: (jnp.where(i < p_tiles, 0, 1 + (i - p_tiles) % per_seq), 0))
    kv = pl.BlockSpec((tm, KV_DIM), lambda i: (i, 0))
    kv_prompt = pl.BlockSpec((tm, KV_DIM), lambda i: (jnp.minimum(i, p_tiles - 1), 0))
    return pl.pallas_call(
        functools.partial(_qkv_kernel, tm=tm, sub=256, p_tiles=p_tiles),
        grid=(lay.t // tm,),
        in_specs=[tile, _mod_spec(lay, tm, 3), _mod_spec(lay, tm, 4), _vec_spec(l * 6 + 2),
                  _resident((None, D_MODEL, Q_DIM + 2 * KV_DIM), (j, 0, 0)), tab, tab],
        out_specs=[pl.BlockSpec((tm, Q_DIM), lambda i: (i, 0)), kv_prompt, kv_prompt, kv, kv],
        out_shape=[jax.ShapeDtypeStruct((lay.t, Q_DIM), BF16),
                   jax.ShapeDtypeStruct((lay.tp, KV_DIM), F32),
                   jax.ShapeDtypeStruct((lay.tp, KV_DIM), F32),
                   jax.ShapeDtypeStruct((lay.t, KV_DIM), BF16),
                   jax.ShapeDtypeStruct((lay.t, KV_DIM), BF16)],
        compiler_params=_params(40),
        name="qkv_rope",
    )(x, mod_l, mod_l, norm_w, w_qkv, cos_t, sin_t)


def _attend(q_ref, sink_ref, o_scr, s_scr, blocks, j):
    rowblk = lax.broadcasted_iota(jnp.int32, (GROUP * BLOCK, 1), 0) // BLOCK
    low_lanes = lax.broadcasted_iota(jnp.int32, (BLOCK, 2 * HEAD_DIM), 1) < HEAD_DIM
    n_slots = s_scr.shape[0]

    def score_pass(u, blk, kvh):
        lanes = slice(kvh * HEAD_DIM, (kvh + 1) * HEAD_DIM)
        q = jnp.concatenate(
            [q_ref[blk["rows"], (kvh * GROUP + g) * HEAD_DIM:(kvh * GROUP + g + 1) * HEAD_DIM]
             for g in range(GROUP)], axis=0)
        sink = jnp.zeros((GROUP * BLOCK, 1), F32)
        for g in range(GROUP):
            sink = jnp.where(rowblk == g, sink_ref[j, kvh * GROUP + g] * LOG2E, sink)
        cols, lo = [], 0
        for k_ref, k_rows, mask in blk["keys"]:
            s = lax.dot_general(q, k_ref[k_rows, lanes], (((1,), (1,)), ((), ())), preferred_element_type=F32)
            if mask is not None:
                s = jnp.where(mask, s, NEG_INF)
            s_scr[u % n_slots, :, lo:lo + s.shape[1]] = s
            lo += s.shape[1]
            cols += [s[:, c * BLOCK:(c + 1) * BLOCK] for c in range(s.shape[1] // BLOCK)]
        m = jnp.maximum(jnp.max(functools.reduce(jnp.maximum, cols), axis=-1, keepdims=True), sink)
        return sink, m

    def value_pass(u, blk, kvh, sink, m):
        pair = slice((kvh // 2) * 2 * HEAD_DIM, (kvh // 2 + 1) * 2 * HEAD_DIM)
        psum = jnp.zeros((GROUP * BLOCK, BLOCK), F32)
        o2 = jnp.zeros((GROUP * BLOCK, 2 * HEAD_DIM), F32)
        lo = 0
        for group in blk["values"]:
            v = [v_ref[v_rows, pair] for v_ref, v_rows in group]
            v = v[0] if len(v) == 1 else jnp.concatenate(v, axis=0)
            rows = v.shape[0]
            p = jnp.exp2(s_scr[u % n_slots, :, lo:lo + rows] - m)
            lo += rows
            for c in range(rows // BLOCK):
                psum = psum + p[:, c * BLOCK:(c + 1) * BLOCK]
            o2 = o2 + jnp.dot(p.astype(BF16), v, preferred_element_type=F32)
        denom = jnp.sum(psum, axis=-1, keepdims=True) + jnp.exp2(sink - m)
        o2 = o2 * (1.0 / denom)
        for half in range(GROUP // 2):
            a = o2[(2 * half) * BLOCK:(2 * half + 1) * BLOCK]
            b = o2[(2 * half + 1) * BLOCK:(2 * half + 2) * BLOCK]
            if kvh % 2 == 0:
                b = pltpu.roll(b, HEAD_DIM, 1)
            else:
                a = pltpu.roll(a, HEAD_DIM, 1)
            c = kvh * (GROUP // 2) + half
            o_scr[blk["rows"], c * 2 * HEAD_DIM:(c + 1) * 2 * HEAD_DIM] = jnp.where(low_lanes, a, b).astype(BF16)

    units = [(blk, kvh) for blk in blocks for kvh in range(N_KV_HEADS)]
    parked = {}
    for step in range(len(units) + 1):
        if step < len(units):
            parked[step] = score_pass(step, *units[step])
        if step >= 1:
            value_pass(step - 1, *units[step - 1], *parked.pop(step - 1))


def _attn_kernel(sink_ref, x_ref, gt_ref, g3_ref, q_ref, kp_ref, kc_ref, kn_ref, vp_ref, vc_ref, vn_ref,
                 ck_ref, cv_ref, wo_ref, o_ref, o_scr, s_scr, *, p_steps, per_seq, j):
    i = pl.program_id(0)
    is_prompt = i < p_steps
    lo, hi, both = slice(0, BLOCK), slice(BLOCK, 2 * BLOCK), slice(0, 2 * BLOCK)
    full = slice(None)

    @pl.when(is_prompt)
    def _():
        blocks = [dict(rows=rows, keys=[(kc_ref, lo, None), (kc_ref, hi, None)],
                       values=[[(vc_ref, lo)], [(vc_ref, hi)]]) for rows in (lo, hi)]
        _attend(q_ref, sink_ref, o_scr, s_scr, blocks, j)

    @pl.when(jnp.logical_not(is_prompt))
    def _():
        local = (i - p_steps) % per_seq
        r = lax.broadcasted_iota(jnp.int32, (GROUP * BLOCK, BLOCK), 0) % BLOCK
        c = lax.broadcasted_iota(jnp.int32, (GROUP * BLOCK, BLOCK), 1)
        behind, ahead = c >= r, c <= r
        first_mask = jnp.logical_and(behind, local != 0)
        last_mask = jnp.logical_and(ahead, local != per_seq - 1)
        blocks = [
            dict(rows=lo,
                 keys=[(kp_ref, full, first_mask), (kc_ref, lo, None), (kc_ref, hi, ahead), (ck_ref, full, None)],
                 values=[[(vp_ref, full), (vc_ref, both)], [(cv_ref, full)]]),
            dict(rows=hi,
                 keys=[(kc_ref, lo, behind), (kc_ref, hi, None), (kn_ref, full, last_mask), (ck_ref, full, None)],
                 values=[[(vc_ref, both), (vn_ref, full)], [(cv_ref, full)]]),
        ]
        _attend(q_ref, sink_ref, o_scr, s_scr, blocks, j)

    y = jnp.dot(o_scr[...], wo_ref[...], preferred_element_type=F32)
    o_ref[...] = x_ref[...] + gt_ref[...] * _rms(y, g3_ref[...])


def _attention(lay, x, mod_l, norm_w, q, k, v, cache_k, cache_v, sink, w_o, l, j):
    tm = 2 * BLOCK
    assert lay.prompt_len == tm and lay.dec_len % tm == 0
    p_steps = lay.tp // tm
    per_seq = lay.dec_len // tm

    def prev_idx(i):
        local = (i - p_steps) % per_seq
        return jnp.where(jnp.logical_or(i < p_steps, local == 0), 2 * i, 2 * i - 1)

    def next_idx(i):
        local = (i - p_steps) % per_seq
        return jnp.where(jnp.logical_or(i < p_steps, local == per_seq - 1), 2 * i + 1, 2 * i + 2)

    def cache_idx(i):
        return jnp.where(i < p_steps, 0, (i - p_steps) // per_seq)

    tile = pl.BlockSpec((tm, D_MODEL), lambda i: (i, 0))
    kv_prev = pl.BlockSpec((BLOCK, KV_DIM), lambda i: (prev_idx(i), 0))
    kv_cur = pl.BlockSpec((tm, KV_DIM), lambda i: (i, 0))
    kv_next = pl.BlockSpec((BLOCK, KV_DIM), lambda i: (next_idx(i), 0))
    past = cache_k.shape[2]
    ctx = pl.BlockSpec((None, None, past, KV_DIM), lambda i: (cache_idx(i), j, 0, 0))
    return pl.pallas_call(
        functools.partial(_attn_kernel, p_steps=p_steps, per_seq=per_seq, j=j),
        grid=(lay.t // tm,),
        in_specs=[pl.BlockSpec(memory_space=pltpu.SMEM),
                  tile, _mod_spec(lay, tm, 5), _vec_spec(l * 6 + 3),
                  pl.BlockSpec((tm, Q_DIM), lambda i: (i, 0)),
                  kv_prev, kv_cur, kv_next, kv_prev, kv_cur, kv_next, ctx, ctx,
                  _resident((None, Q_DIM, D_MODEL), (j, 0, 0))],
        out_specs=tile,
        out_shape=jax.ShapeDtypeStruct((lay.t, D_MODEL), F32),
        scratch_shapes=[pltpu.VMEM((tm, Q_DIM), BF16),
                        pltpu.VMEM((N_KV_HEADS, GROUP * BLOCK, 3 * BLOCK + past), F32)],
        compiler_params=_params(32),
        name="window_attention",
    )(sink, x, mod_l, norm_w, q, k, k, k, v, v, v, cache_k, cache_v, w_o)


def kernel(x_prompt, x_sample, cache_k, cache_v, c, c_ctx, w_mod, b_mod, norm_w, ffn_w_gate, ffn_w_up,
           ffn_w_down, cm_w_in, cm_conv_w, cm_conv_b, cm_conv_ln_g, cm_conv_ln_b, cm_sgu_ln_g, cm_sgu_ln_b,
           cm_sgu_w, cm_sgu_b, cm_w_out, attn_w_qkv, attn_w_o, attn_sink):
    n_prompt, prompt_len, _ = x_prompt.shape
    n_dec, dec_len, _ = x_sample.shape
    lay = _Layout(n_prompt, prompt_len, n_dec, dec_len)
    n_attn = attn_w_qkv.shape[0]
    n_cm = cm_w_in.shape[0]
    past = cache_k.shape[2]

    x = (x_prompt.reshape(lay.tp, D_MODEL), x_sample.reshape(lay.ts, D_MODEL))
    cond =jnp.concatenate([c_ctx[None, :], c, jnp.zeros((MOD_ROWS - 1 - n_dec, D_MODEL), F32)], axis=0)
    mod = _modulation(cond, w_mod, b_mod).reshape(DEPTH, MOD_ROWS, 1, N_MOD * D_MODEL)
    norm_rows = norm_w.reshape(DEPTH * 6, 1, D_MODEL)

    wg = ffn_w_gate.astype(BF16)
    wu = ffn_w_up.astype(BF16)
    wd = ffn_w_down.astype(BF16)
    w_in = cm_w_in.astype(BF16)
    w_out = cm_w_out.astype(BF16)
    sgu_w = cm_sgu_w.astype(BF16)
    sgu_bt = jnp.swapaxes(cm_sgu_b, 1, 2)
    w_qkv = attn_w_qkv.astype(BF16)
    w_o = attn_w_o.astype(BF16)
    vec_cm = lambda a: a.reshape(n_cm, 1, -1)
    conv_w = jnp.broadcast_to(cm_conv_w[:, :, None, :], (n_cm, CONV_WIDTH, SUBLANES, CONV_CH))
    ck = cache_k.reshape(n_dec, n_attn, past, KV_DIM).astype(BF16)
    cv = cache_v.reshape(n_dec, n_attn, past, KV_DIM).astype(BF16)
    qkv_tm = 1024
    cos_t, sin_t = _rope_tables(dec_len, qkv_tm)

    new_k, new_v = [], []
    for l in range(DEPTH):
        mod_l = mod[l]
        j = l // 2
        x = _ffn(lay, x, mod_l, norm_rows, wg, wu, wd, l, 0, 0)
        if l % 2 == 0:
            g, b_out = _cm_in(lay, x, mod_l, norm_rows, w_in, vec_cm(cm_sgu_ln_g), vec_cm(cm_sgu_ln_b),
                              sgu_w, sgu_bt, l, j)
            x = _cm_out(lay, x, mod_l, norm_rows, g, b_out, conv_w, vec_cm(cm_conv_b),
                        vec_cm(cm_conv_ln_g), vec_cm(cm_conv_ln_b), w_out, l, j)
        else:
            q, k, v, kb, vb = _qkv(lay, x, mod_l, norm_rows, w_qkv, cos_t, sin_t, l, j, qkv_tm)
            new_k.append(k.reshape(n_prompt, prompt_len, N_KV_HEADS, HEAD_DIM))
            new_v.append(v.reshape(n_prompt, prompt_len, N_KV_HEADS, HEAD_DIM))
            x = _attention(lay, x, mod_l, norm_rows, q, kb, vb, ck, cv, attn_sink, w_o, l, j)
        x = _ffn(lay, x, mod_l, norm_rows, wg, wu, wd, l, 1, 2, split_out=(l == DEPTH - 1))
    y_prompt, y_sample = x
    return (y_prompt.reshape(x_prompt.shape), y_sample.reshape(x_sample.shape),
            jnp.stack(new_k, axis=1), jnp.stack(new_v, axis=1))
```

```python
import functools

import numpy as np
import jax
import jax.numpy as jnp
from jax import lax
from jax.experimental import pallas as pl
from jax.experimental.pallas import tpu as pltpu

F32 = jnp.float32
BF16 = jnp.bfloat16

D_MODEL = 1024
DEPTH = 4
GRID_W = 64
N_MOD = 9
D_FF = 2816
CONV_CH = 512
CONV_WIDTH = 31
CONV_HALF = CONV_WIDTH // 2
SGU_CH = 512
SGU_GROUPS = 4
CHUNK = 128
HEAD_DIM = 64
N_HEADS = 16
N_KV_HEADS = 4
GROUP = N_HEADS // N_KV_HEADS
Q_DIM = N_HEADS * HEAD_DIM
KV_DIM = N_KV_HEADS * HEAD_DIM
BLOCK = 128
ROPE_HALF = HEAD_DIM // 2
ROPE_BASE = 10000.0
EPS = 1e-6
NEG_INF = -1e30
ATTN_SCALE = HEAD_DIM ** -0.5
LOG2E = float(np.log2(np.e))

SUBLANES = 8
MOD_ROWS = 8
HALO = 16
MIB = 1024 * 1024


def _rms(x, g):
    return x * lax.rsqrt(jnp.mean(x * x, axis=-1, keepdims=True) + EPS) * g


def _ln(x, g, b):
    mu = jnp.mean(x, axis=-1, keepdims=True)
    xc = x - mu
    var = jnp.mean(xc * xc, axis=-1, keepdims=True)
    return xc * lax.rsqrt(var + EPS) * g + b


def _resident(block_shape, index):
    return pl.BlockSpec(block_shape, lambda i: index, pipeline_mode=pl.Buffered(1))


def _params(vmem_mib):
    return pltpu.CompilerParams(dimension_semantics=("arbitrary",),
                                vmem_limit_bytes=vmem_mib * MIB)


class _Layout:
    def __init__(self, n_prompt, prompt_len, n_dec, dec_len):
        self.n_prompt, self.prompt_len = n_prompt, prompt_len
        self.n_dec, self.dec_len = n_dec, dec_len
        self.tp = n_prompt * prompt_len
        self.ts = n_dec * dec_len
        self.t = self.tp + self.ts

    def mod_row(self, i, tm):
        p_tiles = self.tp // tm
        per_seq = self.dec_len // tm
        return jnp.where(i < p_tiles, 0, 1 + (i - p_tiles) // per_seq)


def _vec_spec(row):
    return pl.BlockSpec((None, 1, D_MODEL), lambda i: (row, 0, 0))


def _mod_spec(lay, tm, col):
    return pl.BlockSpec((None, 1, D_MODEL), lambda i: (lay.mod_row(i, tm), 0, col))


def _mod_kernel(c_ref, w_ref, b_ref, o_ref):
    c = c_ref[...]
    s = (c * jax.nn.sigmoid(c)).astype(BF16)
    o_ref[...] = jnp.dot(s, w_ref[...].astype(BF16), preferred_element_type=F32) + b_ref[...]


def _modulation(cond, w_mod, b_mod):
    tn = 1024
    n = N_MOD * D_MODEL
    return pl.pallas_call(
        _mod_kernel,
        grid=(DEPTH, n // tn),
        in_specs=[pl.BlockSpec((MOD_ROWS, D_MODEL), lambda l, j: (0, 0)),
                  pl.BlockSpec((None, D_MODEL, tn), lambda l, j: (l, 0, j)),
                  pl.BlockSpec((None, 1, tn), lambda l, j: (l, 0, j))],
        out_specs=pl.BlockSpec((None, MOD_ROWS, tn), lambda l, j: (l, 0, j)),
        out_shape=jax.ShapeDtypeStruct((DEPTH, MOD_ROWS, n), F32),
        compiler_params=pltpu.CompilerParams(dimension_semantics=("arbitrary", "arbitrary"),
                                             vmem_limit_bytes=24 * MIB),
        name="modulation",
    )(cond, w_mod, b_mod.reshape(DEPTH, 1, n))


def _ffn_kernel(*refs, weight, p_tiles, split_in, split_out, sub):
    n_x = 2 if split_in else 1
    x_refs, o_refs = refs[:n_x], refs[n_x + 8:]
    sh_ref, sc_ref, gt_ref, g1_ref, g2_ref, wg_ref, wu_ref, wd_ref = refs[n_x:n_x + 8]
    is_prompt = pl.program_id(0) < p_tiles
    tm = x_refs[0].shape[0]
    n_sub = tm // sub
    xs, ab, ys, outs = {}, {}, {}, []

    def load_and_up(r):
        rows = slice(r * sub, (r + 1) * sub)
        if split_in:
            x = jnp.where(is_prompt, x_refs[0][rows, :], x_refs[1][rows, :])
        else:
            x = x_refs[0][rows, :]
        h = (_rms(x, g1_ref[...]) * (1.0 + sc_ref[...]) + sh_ref[...]).astype(BF16)
        xs[r] = x
        ab[r] = (jnp.dot(h, wg_ref[...], preferred_element_type=F32),
                 jnp.dot(h, wu_ref[...], preferred_element_type=F32))

    def gate_and_down(r):
        a, b = ab.pop(r)
        act = (a * jax.nn.sigmoid(a) * b).astype(BF16)
        ys[r] = jnp.dot(act, wd_ref[...], preferred_element_type=F32)

    def finish(r):
        outs.append(xs.pop(r) + weight * gt_ref[...] * _rms(ys.pop(r), g2_ref[...]))
        if not split_out:
            o_refs[0][r * sub:(r + 1) * sub, :] = outs[-1]

    for step in range(n_sub + 2):
        if step < n_sub:
            load_and_up(step)
        if 0 <= step - 1 < n_sub:
            gate_and_down(step - 1)
        if 0 <= step - 2 < n_sub:
            finish(step - 2)
    if split_out:
        out = jnp.concatenate(outs, axis=0)

        @pl.when(is_prompt)
        def _():
            o_refs[0][...] = out

        @pl.when(jnp.logical_not(is_prompt))
        def _():
            o_refs[1][...] = out


def _ffn(lay, xs, mod_l, norm_w, wg, wu, wd, l, s, slot, split_out=False):
    split_in = isinstance(xs, tuple)
    tm = 1024
    p_tiles = lay.tp // tm
    tile = pl.BlockSpec((tm, D_MODEL), lambda i: (i, 0))
    p_tile = pl.BlockSpec((tm, D_MODEL), lambda i: (jnp.minimum(i, p_tiles - 1), 0))
    s_tile = pl.BlockSpec((tm, D_MODEL), lambda i: (jnp.maximum(i - p_tiles, 0), 0))
    xs = xs if split_in else (xs,)
    if split_out:
        out_specs = [p_tile, s_tile]
        out_shape = [jax.ShapeDtypeStruct((lay.tp, D_MODEL), F32), jax.ShapeDtypeStruct((lay.ts, D_MODEL), F32)]
    else:
        out_specs, out_shape = tile, jax.ShapeDtypeStruct((lay.t, D_MODEL), F32)
    return pl.pallas_call(
        functools.partial(_ffn_kernel, weight=0.5, p_tiles=p_tiles, split_in=split_in, split_out=split_out,
                          sub=256),
        grid=(lay.t // tm,),
        in_specs=([p_tile, s_tile] if split_in else [tile]) + [
            _mod_spec(lay, tm, 3 * slot), _mod_spec(lay, tm, 3 * slot + 1), _mod_spec(lay, tm, 3 * slot + 2),
            _vec_spec(l * 6 + 2 * slot), _vec_spec(l * 6 + 2 * slot + 1),
            _resident((None, None, D_MODEL, D_FF), (l, s, 0, 0)),
            _resident((None, None, D_MODEL, D_FF), (l, s, 0, 0)),
            _resident((None, None, D_FF, D_MODEL), (l, s, 0, 0))],
        out_specs=out_specs,
        out_shape=out_shape,
        compiler_params=_params(56 if (split_in or split_out) else 52),
        name="swiglu",
    )(*xs, mod_l, mod_l, mod_l, norm_w, norm_w, wg, wu, wd)


def _cm_in_kernel(x_ref, sh_ref, sc_ref, g_ref, win_ref, slg_ref, slb_ref, sw_ref, sb_ref,
                  gout_ref, bout_ref, *, tm, sub):
    gc = SGU_CH // SGU_GROUPS
    n_sub = tm // sub
    zs = {}

    def project(r):
        x = x_ref[r * sub:(r + 1) * sub, :]
        h = (_rms(x, g_ref[...]) * (1.0 + sc_ref[...]) + sh_ref[...]).astype(BF16)
        zs[r] = jnp.dot(h, win_ref[...], preferred_element_type=F32)

    project(0)
    for r in range(n_sub):
        if r + 1 < n_sub:
            project(r + 1)
        z = zs.pop(r)
        a_val = z[:, :CONV_CH]
        a_gate = z[:, CONV_CH:2 * CONV_CH]
        u = jax.nn.gelu(z[:, 2 * CONV_CH:2 * CONV_CH + SGU_CH])
        v = _ln(jax.nn.gelu(z[:, 2 * CONV_CH + SGU_CH:]), slg_ref[...], slb_ref[...]).astype(BF16)
        gout_ref[r * sub:(r + 1) * sub, :] = a_val * jax.nn.sigmoid(a_gate)
        for n in range(sub // CHUNK):
            rows = slice(n * CHUNK, (n + 1) * CHUNK)
            out_rows = slice(r * sub + n * CHUNK, r * sub + (n + 1) * CHUNK)
            for gi in range(SGU_GROUPS):
                cols = slice(gi * gc, (gi + 1) * gc)
                vc = jnp.dot(sw_ref[gi], v[rows, cols], preferred_element_type=F32) + sb_ref[:, gi:gi + 1]
                bout_ref[out_rows, cols] = u[rows, cols] * vc


def _cm_in(lay, x, mod_l, norm_w, w_in, sln_g, sln_b, sgu_w, sgu_bt, l, j):
    tm = 1024
    tile = pl.BlockSpec((tm, D_MODEL), lambda i: (i, 0))
    half = pl.BlockSpec((tm, CONV_CH), lambda i: (i, 0))
    n_in = 2 * CONV_CH + 2 * SGU_CH
    return pl.pallas_call(
        functools.partial(_cm_in_kernel, tm=tm, sub=256),
        grid=(lay.t // tm,),
        in_specs=[tile, _mod_spec(lay, tm, 3), _mod_spec(lay, tm, 4), _vec_spec(l * 6 + 2),
                  _resident((None, D_MODEL, n_in), (j, 0, 0)),
                  pl.BlockSpec((None, 1, SGU_CH), lambda i: (j, 0, 0)),
                  pl.BlockSpec((None, 1, SGU_CH), lambda i: (j, 0, 0)),
                  _resident((None, SGU_GROUPS, CHUNK, CHUNK), (j, 0, 0, 0)),
                  pl.BlockSpec((None, CHUNK, SGU_GROUPS), lambda i: (j, 0, 0))],
        out_specs=[half, half],
        out_shape=[jax.ShapeDtypeStruct((lay.t, CONV_CH), F32),
                   jax.ShapeDtypeStruct((lay.t, SGU_CH), F32)],
        compiler_params=_params(40),
        name="conv_gmlp_in",
    )(x, mod_l, mod_l, norm_w, w_in, sln_g, sln_b, sgu_w, sgu_bt)


def _cm_out_kernel(x_ref, gt_ref, g3_ref, gp_ref, gc_ref, gn_ref, bo_ref, cw_ref, cb_ref,
                   clg_ref, clb_ref, wout_ref, o_ref, gbuf, sbuf, abuf, *, tm, win, p_tiles, per_seq):
    i = pl.program_id(0)
    is_latent = i >= p_tiles
    local = (i - p_tiles) % per_seq
    n_win = tm // win
    span = win + 2 * HALO - SUBLANES
    rb = 32
    for w in range(n_win):
        if w == 0:
            left = jnp.where(jnp.logical_and(is_latent, local != 0), gp_ref[...], 0.0)
        else:
            left = jnp.where(is_latent, gc_ref[w * win - HALO:w * win, :], 0.0)
        if w == n_win - 1:
            right = jnp.where(jnp.logical_and(is_latent, local != per_seq - 1), gn_ref[...], 0.0)
        else:
            right = jnp.where(is_latent, gc_ref[(w + 1) * win:(w + 1) * win + HALO, :], 0.0)
        gbuf[0:HALO, :] = left
        gbuf[HALO:HALO + win, :] = gc_ref[w * win:(w + 1) * win, :]
        gbuf[HALO + win:, :] = right
        for s in range(1, SUBLANES):
            sbuf[s - 1] = gbuf[s:s + span, :]
        for r in range(win // rb):
            acc = jnp.broadcast_to(cb_ref[...], (rb // SUBLANES, SUBLANES, CONV_CH))
            for k in range(CONV_WIDTH):
                off = HALO - CONV_HALF + k
                s, base = off % SUBLANES, r * rb + off - off % SUBLANES
                src = gbuf[base:base + rb, :] if s == 0 else sbuf[s - 1, base:base + rb, :]
                acc = acc + cw_ref[k][None] * src.reshape(rb // SUBLANES, SUBLANES, CONV_CH)
            a = _ln(acc.reshape(rb, CONV_CH), clg_ref[...], clb_ref[...])
            abuf[w * win + r * rb:w * win + (r + 1) * rb, :] = (a * jax.nn.sigmoid(a)).astype(BF16)
    y = jnp.dot(abuf[...], wout_ref[:CONV_CH, :], preferred_element_type=F32)
    y = y + jnp.dot(bo_ref[...].astype(BF16), wout_ref[CONV_CH:, :], preferred_element_type=F32)
    o_ref[...] = x_ref[...] + gt_ref[...] * _rms(y, g3_ref[...])


def _cm_out(lay, x, mod_l, norm_w, g, b_out, conv_w, conv_b, cln_g, cln_b, w_out, l, j):
    win = lay.prompt_len
    tm = 2 * win
    assert lay.tp % tm == 0 and lay.dec_len % tm == 0 and win % HALO == 0
    hb = tm // HALO
    last = lay.t // HALO - 1
    tile = pl.BlockSpec((tm, D_MODEL), lambda i: (i, 0))
    half = pl.BlockSpec((tm, CONV_CH), lambda i: (i, 0))
    vec = pl.BlockSpec((None, 1, CONV_CH), lambda i: (j, 0, 0))
    return pl.pallas_call(
        functools.partial(_cm_out_kernel, tm=tm, win=win, p_tiles=lay.tp // tm, per_seq=lay.dec_len // tm),
        grid=(lay.t // tm,),
        in_specs=[tile, _mod_spec(lay, tm, 5), _vec_spec(l * 6 + 3),
                  pl.BlockSpec((HALO, CONV_CH), lambda i: (jnp.maximum(i * hb - 1, 0), 0)),
                  half,
                  pl.BlockSpec((HALO, CONV_CH), lambda i: (jnp.minimum((i + 1) * hb, last), 0)),
                  half,
                  pl.BlockSpec((None, CONV_WIDTH, SUBLANES, CONV_CH), lambda i: (j, 0, 0, 0)),
                  vec, vec, vec,
                  _resident((None, CONV_CH + SGU_CH, D_MODEL), (j, 0, 0))],
        out_specs=tile,
        out_shape=jax.ShapeDtypeStruct((lay.t, D_MODEL), F32),
        scratch_shapes=[pltpu.VMEM((win + 2 * HALO, CONV_CH), F32),
                        pltpu.VMEM((SUBLANES - 1, win + 2 * HALO - SUBLANES, CONV_CH), F32),
                        pltpu.VMEM((tm, CONV_CH), BF16)],
        compiler_params=_params(32),
        name="conv_gmlp_out",
    )(x, mod_l, norm_w, g, g, g, b_out, conv_w, conv_b, cln_g, cln_b, w_out)


def _rope_tables(dec_len, pad_rows):
    pos = np.arange(dec_len)
    r = (pos // GRID_W).astype(np.float64)
    c = (pos % GRID_W).astype(np.float64)
    inv = np.power(ROPE_BASE, -np.arange(0, ROPE_HALF, 2, dtype=np.float64) / ROPE_HALF)
    ang_r = r[:, None] * inv[None, :]
    ang_c = c[:, None] * inv[None, :]
    cos = np.concatenate([np.cos(ang_r)] * 2 + [np.cos(ang_c)] * 2, axis=-1)
    sin = np.concatenate([-np.sin(ang_r), np.sin(ang_r), -np.sin(ang_c), np.sin(ang_c)], axis=-1)
    cos = np.concatenate([np.ones((pad_rows, HEAD_DIM)), cos], axis=0)
    sin = np.concatenate([np.zeros((pad_rows, HEAD_DIM)), sin], axis=0)
    return (jnp.asarray(np.tile(cos, (1, 2)), F32), jnp.asarray(np.tile(sin, (1, 2)), F32))


def _rope(z, cos, sin, first_half):
    partner = jnp.where(first_half, pltpu.roll(z, 128 - ROPE_HALF // 2, 1), pltpu.roll(z, ROPE_HALF // 2, 1))
    return z * cos + partner * sin


def _qkv_kernel(x_ref, sh_ref, sc_ref, g_ref, w_ref, cos_ref, sin_ref, q_ref, k_ref, v_ref, kb_ref, vb_ref,
                *, tm, sub, p_tiles):
    lane = lax.broadcasted_iota(jnp.int32, (sub, 128), 1)
    first_half = (lane % ROPE_HALF) < (ROPE_HALF // 2)
    n_sub = tm // sub
    zs, ks, vs = {}, [], []

    def project(r):
        x = x_ref[r * sub:(r + 1) * sub, :]
        h = (_rms(x, g_ref[...]) * (1.0 + sc_ref[...]) + sh_ref[...]).astype(BF16)
        zs[r] = jnp.dot(h, w_ref[...], preferred_element_type=F32)

    project(0)
    for r in range(n_sub):
        if r + 1 < n_sub:
            project(r + 1)
        z = zs.pop(r)
        rows = slice(r * sub, (r + 1) * sub)
        cos = cos_ref[rows, :]
        sin = sin_ref[rows, :]
        for c in range(Q_DIM // 128):
            zc = _rope(z[:, c * 128:(c + 1) * 128], cos, sin, first_half)
            q_ref[rows, c * 128:(c + 1) * 128] = (zc * (ATTN_SCALE * LOG2E)).astype(BF16)
        k = jnp.concatenate([_rope(z[:, Q_DIM + c * 128:Q_DIM + (c + 1) * 128], cos, sin, first_half)
                             for c in range(KV_DIM // 128)], axis=1)
        v = z[:, Q_DIM + KV_DIM:]
        kb_ref[rows, :] = k.astype(BF16)
        vb_ref[rows, :] = v.astype(BF16)
        ks.append(k)
        vs.append(v)

    @pl.when(pl.program_id(0) < p_tiles)
    def _():
        k_ref[...] = jnp.concatenate(ks, axis=0)
        v_ref[...] = jnp.concatenate(vs, axis=0)


def _qkv(lay, x, mod_l, norm_w, w_qkv, cos_t, sin_t, l, j, tm):
    p_tiles = lay.tp // tm
    per_seq = lay.dec_len // tm
    tile = pl.BlockSpec((tm, D_MODEL), lambda i: (i, 0))
    tab = pl.BlockSpec((tm, 128), lambda i: (jnp.where(i < p_tiles, 0, 1 + (i - p_tiles) % per_seq), 0))
    kv = pl.BlockSpec((tm, KV_DIM), lambda i: (i, 0))
    kv_prompt = pl.BlockSpec((tm, KV_DIM), lambda i: (jnp.minimum(i, p_tiles - 1), 0))
    return pl.pallas_call(
        functools.partial(_qkv_kernel, tm=tm, sub=256, p_tiles=p_tiles),
        grid=(lay.t // tm,),
        in_specs=[tile, _mod_spec(lay, tm, 3), _mod_spec(lay, tm, 4), _vec_spec(l * 6 + 2),
                  _resident((None, D_MODEL, Q_DIM + 2 * KV_DIM), (j, 0, 0)), tab, tab],
        out_specs=[pl.BlockSpec((tm, Q_DIM), lambda i: (i, 0)), kv_prompt, kv_prompt, kv, kv],
        out_shape=[jax.ShapeDtypeStruct((lay.t, Q_DIM), BF16),
                   jax.ShapeDtypeStruct((lay.tp, KV_DIM), F32),
                   jax.ShapeDtypeStruct((lay.tp, KV_DIM), F32),
                   jax.ShapeDtypeStruct((lay.t, KV_DIM), BF16),
                   jax.ShapeDtypeStruct((lay.t, KV_DIM), BF16)],
        compiler_params=_params(40),
        name="qkv_rope",
    )(x, mod_l, mod_l, norm_w, w_qkv, cos_t, sin_t)


def _attend(q_ref, sink_ref, o_scr, s_scr, blocks, j):
    rowblk = lax.broadcasted_iota(jnp.int32, (GROUP * BLOCK, 1), 0) // BLOCK
    low_lanes = lax.broadcasted_iota(jnp.int32, (BLOCK, 2 * HEAD_DIM), 1) < HEAD_DIM
    n_slots = s_scr.shape[0]

    def score_pass(u, blk, kvh):
        lanes = slice(kvh * HEAD_DIM, (kvh + 1) * HEAD_DIM)
        q = jnp.concatenate(
            [q_ref[blk["rows"], (kvh * GROUP + g) * HEAD_DIM:(kvh * GROUP + g + 1) * HEAD_DIM]
             for g in range(GROUP)], axis=0)
        sink = jnp.zeros((GROUP * BLOCK, 1), F32)
        for g in range(GROUP):
            sink = jnp.where(rowblk == g, sink_ref[j, kvh * GROUP + g] * LOG2E, sink)
        cols, lo = [], 0
        for k_ref, k_rows, mask in blk["keys"]:
            s = lax.dot_general(q, k_ref[k_rows, lanes], (((1,), (1,)), ((), ())), preferred_element_type=F32)
            if mask is not None:
                s = jnp.where(mask, s, NEG_INF)
            s_scr[u % n_slots, :, lo:lo + s.shape[1]] = s
            lo += s.shape[1]
            cols += [s[:, c * BLOCK:(c + 1) * BLOCK] for c in range(s.shape[1] // BLOCK)]
        m = jnp.maximum(jnp.max(functools.reduce(jnp.maximum, cols), axis=-1, keepdims=True), sink)
        return sink, m

    def value_pass(u, blk, kvh, sink, m):
        pair = slice((kvh // 2) * 2 * HEAD_DIM, (kvh // 2 + 1) * 2 * HEAD_DIM)
        psum = jnp.zeros((GROUP * BLOCK, BLOCK), F32)
        o2 = jnp.zeros((GROUP * BLOCK, 2 * HEAD_DIM), F32)
        lo = 0
        for group in blk["values"]:
            v = [v_ref[v_rows, pair] for v_ref, v_rows in group]
            v = v[0] if len(v) == 1 else jnp.concatenate(v, axis=0)
            rows = v.shape[0]
            p = jnp.exp2(s_scr[u % n_slots, :, lo:lo + rows] - m)
            lo += rows
            for c in range(rows // BLOCK):
                psum = psum + p[:, c * BLOCK:(c + 1) * BLOCK]
            o2 = o2 + jnp.dot(p.astype(BF16), v, preferred_element_type=F32)
        denom = jnp.sum(psum, axis=-1, keepdims=True) + jnp.exp2(sink - m)
        o2 = o2 * (1.0 / denom)
        for half in range(GROUP // 2):
            a = o2[(2 * half) * BLOCK:(2 * half + 1) * BLOCK]
            b = o2[(2 * half + 1) * BLOCK:(2 * half + 2) * BLOCK]
            if kvh % 2 == 0:
                b = pltpu.roll(b, HEAD_DIM, 1)
            else:
                a = pltpu.roll(a, HEAD_DIM, 1)
            c = kvh * (GROUP // 2) + half
            o_scr[blk["rows"], c * 2 * HEAD_DIM:(c + 1) * 2 * HEAD_DIM] = jnp.where(low_lanes, a, b).astype(BF16)

    units = [(blk, kvh) for blk in blocks for kvh in range(N_KV_HEADS)]
    parked = {}
    for step in range(len(units) + 1):
        if step < len(units):
            parked[step] = score_pass(step, *units[step])
        if step >= 1:
            value_pass(step - 1, *units[step - 1], *parked.pop(step - 1))


def _attn_kernel(sink_ref, x_ref, gt_ref, g3_ref, q_ref, kp_ref, kc_ref, kn_ref, vp_ref, vc_ref, vn_ref,
                 ck_ref, cv_ref, wo_ref, o_ref, o_scr, s_scr, *, p_steps, per_seq, j):
    i = pl.program_id(0)
    is_prompt = i < p_steps
    lo, hi, both = slice(0, BLOCK), slice(BLOCK, 2 * BLOCK), slice(0, 2 * BLOCK)
    full = slice(None)

    @pl.when(is_prompt)
    def _():
        blocks = [dict(rows=rows, keys=[(kc_ref, lo, None), (kc_ref, hi, None)],
                       values=[[(vc_ref, lo)], [(vc_ref, hi)]]) for rows in (lo, hi)]
        _attend(q_ref, sink_ref, o_scr, s_scr, blocks, j)

    @pl.when(jnp.logical_not(is_prompt))
    def _():
        local = (i - p_steps) % per_seq
        r = lax.broadcasted_iota(jnp.int32, (GROUP * BLOCK, BLOCK), 0) % BLOCK
        c = lax.broadcasted_iota(jnp.int32, (GROUP * BLOCK, BLOCK), 1)
        behind, ahead = c >= r, c <= r
        first_mask = jnp.logical_and(behind, local != 0)
        last_mask = jnp.logical_and(ahead, local != per_seq - 1)
        blocks = [
            dict(rows=lo,
                 keys=[(kp_ref, full, first_mask), (kc_ref, lo, None), (kc_ref, hi, ahead), (ck_ref, full, None)],
                 values=[[(vp_ref, full), (vc_ref, both)], [(cv_ref, full)]]),
            dict(rows=hi,
                 keys=[(kc_ref, lo, behind), (kc_ref, hi, None), (kn_ref, full, last_mask), (ck_ref, full, None)],
                 values=[[(vc_ref, both), (vn_ref, full)], [(cv_ref, full)]]),
        ]
        _attend(q_ref, sink_ref, o_scr, s_scr, blocks, j)

    y = jnp.dot(o_scr[...], wo_ref[...], preferred_element_type=F32)
    o_ref[...] = x_ref[...] + gt_ref[...] * _rms(y, g3_ref[...])


def _attention(lay, x, mod_l, norm_w, q, k, v, cache_k, cache_v, sink, w_o, l, j):
    tm = 2 * BLOCK
    assert lay.prompt_len == tm and lay.dec_len % tm == 0
    p_steps = lay.tp // tm
    per_seq = lay.dec_len // tm

    def prev_idx(i):
        local = (i - p_steps) % per_seq
        return jnp.where(jnp.logical_or(i < p_steps, local == 0), 2 * i, 2 * i - 1)

    def next_idx(i):
        local = (i - p_steps) % per_seq
        return jnp.where(jnp.logical_or(i < p_steps, local == per_seq - 1), 2 * i + 1, 2 * i + 2)

    def cache_idx(i):
        return jnp.where(i < p_steps, 0, (i - p_steps) // per_seq)

    tile = pl.BlockSpec((tm, D_MODEL), lambda i: (i, 0))
    kv_prev = pl.BlockSpec((BLOCK, KV_DIM), lambda i: (prev_idx(i), 0))
    kv_cur = pl.BlockSpec((tm, KV_DIM), lambda i: (i, 0))
    kv_next = pl.BlockSpec((BLOCK, KV_DIM), lambda i: (next_idx(i), 0))
    past = cache_k.shape[2]
    ctx = pl.BlockSpec((None, None, past, KV_DIM), lambda i: (cache_idx(i), j, 0, 0))
    return pl.pallas_call(
        functools.partial(_attn_kernel, p_steps=p_steps, per_seq=per_seq, j=j),
        grid=(lay.t // tm,),
        in_specs=[pl.BlockSpec(memory_space=pltpu.SMEM),
                  tile, _mod_spec(lay, tm, 5), _vec_spec(l * 6 + 3),
                  pl.BlockSpec((tm, Q_DIM), lambda i: (i, 0)),
                  kv_prev, kv_cur, kv_next, kv_prev, kv_cur, kv_next, ctx, ctx,
                  _resident((None, Q_DIM, D_MODEL), (j, 0, 0))],
        out_specs=tile,
        out_shape=jax.ShapeDtypeStruct((lay.t, D_MODEL), F32),
        scratch_shapes=[pltpu.VMEM((tm, Q_DIM), BF16),
                        pltpu.VMEM((N_KV_HEADS, GROUP * BLOCK, 3 * BLOCK + past), F32)],
        compiler_params=_params(32),
        name="window_attention",
    )(sink, x, mod_l, norm_w, q, k, k, k, v, v, v, cache_k, cache_v, w_o)


def kernel(x_prompt, x_sample, cache_k, cache_v, c, c_ctx, w_mod, b_mod, norm_w, ffn_w_gate, ffn_w_up,
           ffn_w_down, cm_w_in, cm_conv_w, cm_conv_b, cm_conv_ln_g, cm_conv_ln_b, cm_sgu_ln_g, cm_sgu_ln_b,
           cm_sgu_w, cm_sgu_b, cm_w_out, attn_w_qkv, attn_w_o, attn_sink):
    n_prompt, prompt_len, _ = x_prompt.shape
    n_dec, dec_len, _ = x_sample.shape
    lay = _Layout(n_prompt, prompt_len, n_dec, dec_len)
    n_attn = attn_w_qkv.shape[0]
    n_cm = cm_w_in.shape[0]
    past = cache_k.shape[2]

    x = (x_prompt.reshape(lay.tp, D_MODEL), x_sample.reshape(lay.ts, D_MODEL))
    cond = jnp.concatenate([c_ctx[None, :], c, jnp.zeros((MOD_ROWS - 1 - n_dec, D_MODEL), F32)], axis=0)
    mod = _modulation(cond, w_mod, b_mod).reshape(DEPTH, MOD_ROWS, 1, N_MOD * D_MODEL)
    norm_rows = norm_w.reshape(DEPTH * 6, 1, D_MODEL)

    wg = ffn_w_gate.astype(BF16)
    wu = ffn_w_up.astype(BF16)
    wd = ffn_w_down.astype(BF16)
    w_in = cm_w_in.astype(BF16)
    w_out = cm_w_out.astype(BF16)
    sgu_w = cm_sgu_w.astype(BF16)
    sgu_bt = jnp.swapaxes(cm_sgu_b, 1, 2)
    w_qkv = attn_w_qkv.astype(BF16)
    w_o = attn_w_o.astype(BF16)
    vec_cm = lambda a: a.reshape(n_cm, 1, -1)
    conv_w = jnp.broadcast_to(cm_conv_w[:, :, None, :], (n_cm, CONV_WIDTH, SUBLANES, CONV_CH))
    ck = cache_k.reshape(n_dec, n_attn, past, KV_DIM).astype(BF16)
    cv = cache_v.reshape(n_dec, n_attn, past, KV_DIM).astype(BF16)
    qkv_tm = 1024
    cos_t, sin_t = _rope_tables(dec_len, qkv_tm)

    new_k, new_v = [], []
    for l in range(DEPTH):
        mod_l = mod[l]
        j = l // 2
        x = _ffn(lay, x, mod_l, norm_rows, wg, wu, wd, l, 0, 0)
        if l % 2 == 0:
            g, b_out = _cm_in(lay, x, mod_l, norm_rows, w_in, vec_cm(cm_sgu_ln_g), vec_cm(cm_sgu_ln_b),
                              sgu_w, sgu_bt, l, j)
            x = _cm_out(lay, x, mod_l, norm_rows, g, b_out, conv_w, vec_cm(cm_conv_b),
                        vec_cm(cm_conv_ln_g), vec_cm(cm_conv_ln_b), w_out, l, j)
        else:
            q, k, v, kb, vb = _qkv(lay, x, mod_l, norm_rows, w_qkv, cos_t, sin_t, l, j, qkv_tm)
            new_k.append(k.reshape(n_prompt, prompt_len, N_KV_HEADS, HEAD_DIM))
            new_v.append(v.reshape(n_prompt, prompt_len, N_KV_HEADS, HEAD_DIM))
            x = _attention(lay, x, mod_l, norm_rows, q, kb, vb, ck, cv, attn_sink, w_o, l, j)
        x = _ffn(lay, x, mod_l, norm_rows, wg, wu, wd, l, 1, 2, split_out=(l == DEPTH - 1))
    y_prompt, y_sample = x
    return (y_prompt.reshape(x_prompt.shape), y_sample.reshape(x_sample.shape),
            jnp.stack(new_k, axis=1), jnp.stack(new_v, axis=1))
```

```python
import functools

import numpy as np
import jax
import jax.numpy as jnp
from jax import lax
from jax.experimental import pallas as pl
from jax.experimental.pallas import tpu as pltpu

F32 = jnp.float32
BF16 = jnp.bfloat16

D_MODEL = 1024
DEPTH = 4
GRID_W = 64
N_MOD = 9
D_FF = 2816
CONV_CH = 512
CONV_WIDTH = 31
CONV_HALF = CONV_WIDTH // 2
SGU_CH = 512
SGU_GROUPS = 4
CHUNK = 128
HEAD_DIM = 64
N_HEADS = 16
N_KV_HEADS = 4
GROUP = N_HEADS // N_KV_HEADS
Q_DIM = N_HEADS * HEAD_DIM
KV_DIM = N_KV_HEADS * HEAD_DIM
BLOCK = 128
ROPE_HALF = HEAD_DIM // 2
ROPE_BASE = 10000.0
EPS = 1e-6
NEG_INF = -1e30
ATTN_SCALE = HEAD_DIM ** -0.5
LOG2E = float(np.log2(np.e))

SUBLANES = 8
MOD_ROWS = 8
HALO = 16
MIB = 1024 * 1024


def _rms(x, g):
    return x * lax.rsqrt(jnp.mean(x * x, axis=-1, keepdims=True) + EPS) * g


def _ln(x, g, b):
    mu = jnp.mean(x, axis=-1, keepdims=True)
    xc = x - mu
    var = jnp.mean(xc * xc, axis=-1, keepdims=True)
    return xc * lax.rsqrt(var + EPS) * g + b


def _resident(block_shape, index):
    return pl.BlockSpec(block_shape, lambda i: index, pipeline_mode=pl.Buffered(1))


def _params(vmem_mib):
    return pltpu.CompilerParams(dimension_semantics=("arbitrary",),
                                vmem_limit_bytes=vmem_mib * MIB)


class _Layout:
    def __init__(self, n_prompt, prompt_len, n_dec, dec_len):
        self.n_prompt, self.prompt_len = n_prompt, prompt_len
        self.n_dec, self.dec_len = n_dec, dec_len
        self.tp = n_prompt * prompt_len
        self.ts = n_dec * dec_len
        self.t = self.tp + self.ts

    def mod_row(self, i, tm):
        p_tiles = self.tp // tm
        per_seq = self.dec_len // tm
        return jnp.where(i < p_tiles, 0, 1 + (i - p_tiles) // per_seq)


def _vec_spec(row):
    return pl.BlockSpec((None, 1, D_MODEL), lambda i: (row, 0, 0))


def _mod_spec(lay, tm, col):
    return pl.BlockSpec((None, 1, D_MODEL), lambda i: (lay.mod_row(i, tm), 0, col))


def _mod_kernel(c_ref, w_ref, b_ref, o_ref):
    c = c_ref[...]
    s = (c * jax.nn.sigmoid(c)).astype(BF16)
    o_ref[...] = jnp.dot(s, w_ref[...].astype(BF16), preferred_element_type=F32) + b_ref[...]


def _modulation(cond, w_mod, b_mod):
    tn = 1024
    n = N_MOD * D_MODEL
    return pl.pallas_call(
        _mod_kernel,
        grid=(DEPTH, n // tn),
        in_specs=[pl.BlockSpec((MOD_ROWS, D_MODEL), lambda l, j: (0, 0)),
                  pl.BlockSpec((None, D_MODEL, tn), lambda l, j: (l, 0, j)),
                  pl.BlockSpec((None, 1, tn), lambda l, j: (l, 0, j))],
        out_specs=pl.BlockSpec((None, MOD_ROWS, tn), lambda l, j: (l, 0, j)),
        out_shape=jax.ShapeDtypeStruct((DEPTH, MOD_ROWS, n), F32),
        compiler_params=pltpu.CompilerParams(dimension_semantics=("arbitrary", "arbitrary"),
                                             vmem_limit_bytes=24 * MIB),
        name="modulation",
    )(cond, w_mod, b_mod.reshape(DEPTH, 1, n))


def _ffn_kernel(*refs, weight, p_tiles, split_in, split_out, sub):
    n_x = 2 if split_in else 1
    x_refs, o_refs = refs[:n_x], refs[n_x + 8:]
    sh_ref, sc_ref, gt_ref, g1_ref, g2_ref, wg_ref, wu_ref, wd_ref = refs[n_x:n_x + 8]
    is_prompt = pl.program_id(0) < p_tiles
    tm = x_refs[0].shape[0]
    n_sub = tm // sub
    xs, ab, ys, outs = {}, {}, {}, []

    def load_and_up(r):
        rows = slice(r * sub, (r + 1) * sub)
        if split_in:
            x = jnp.where(is_prompt, x_refs[0][rows, :], x_refs[1][rows, :])
        else:
            x = x_refs[0][rows, :]
        h = (_rms(x, g1_ref[...]) * (1.0 + sc_ref[...]) + sh_ref[...]).astype(BF16)
        xs[r] = x
        ab[r] = (jnp.dot(h, wg_ref[...], preferred_element_type=F32),
                 jnp.dot(h, wu_ref[...], preferred_element_type=F32))

    def gate_and_down(r):
        a, b = ab.pop(r)
        act = (a * jax.nn.sigmoid(a) * b).astype(BF16)
        ys[r] = jnp.dot(act, wd_ref[...], preferred_element_type=F32)

    def finish(r):
        outs.append(xs.pop(r) + weight * gt_ref[...] * _rms(ys.pop(r), g2_ref[...]))
        if not split_out:
            o_refs[0][r * sub:(r + 1) * sub, :] = outs[-1]

    for step in range(n_sub + 2):
        if step < n_sub:
            load_and_up(step)
        if 0 <= step - 1 < n_sub:
            gate_and_down(step - 1)
        if 0 <= step - 2 < n_sub:
            finish(step - 2)
    if split_out:
        out = jnp.concatenate(outs, axis=0)

        @pl.when(is_prompt)
        def _():
            o_refs[0][...] = out

        @pl.when(jnp.logical_not(is_prompt))
        def _():
            o_refs[1][...] = out


def _ffn(lay, xs, mod_l, norm_w, wg, wu, wd, l, s, slot, split_out=False):
    split_in = isinstance(xs, tuple)
    tm = 1024
    p_tiles = lay.tp // tm
    tile = pl.BlockSpec((tm, D_MODEL), lambda i: (i, 0))
    p_tile = pl.BlockSpec((tm, D_MODEL), lambda i: (jnp.minimum(i, p_tiles - 1), 0))
    s_tile = pl.BlockSpec((tm, D_MODEL), lambda i: (jnp.maximum(i - p_tiles, 0), 0))
    xs = xs if split_in else (xs,)
    if split_out:
        out_specs = [p_tile, s_tile]
        out_shape = [jax.ShapeDtypeStruct((lay.tp, D_MODEL), F32), jax.ShapeDtypeStruct((lay.ts, D_MODEL), F32)]
    else:
        out_specs, out_shape = tile, jax.ShapeDtypeStruct((lay.t, D_MODEL), F32)
    return pl.pallas_call(
        functools.partial(_ffn_kernel, weight=0.5, p_tiles=p_tiles, split_in=split_in, split_out=split_out,
                          sub=256),
        grid=(lay.t // tm,),
        in_specs=([p_tile, s_tile] if split_in else [tile]) + [
            _mod_spec(lay, tm, 3 * slot), _mod_spec(lay, tm, 3 * slot + 1), _mod_spec(lay, tm, 3 * slot + 2),
            _vec_spec(l * 6 + 2 * slot), _vec_spec(l * 6 + 2 * slot + 1),
            _resident((None, None, D_MODEL, D_FF), (l, s, 0, 0)),
            _resident((None, None, D_MODEL, D_FF), (l, s, 0, 0)),
            _resident((None, None, D_FF, D_MODEL), (l, s, 0, 0))],
        out_specs=out_specs,
        out_shape=out_shape,
        compiler_params=_params(56 if (split_in or split_out) else 52),
        name="swiglu",
    )(*xs, mod_l, mod_l, mod_l, norm_w, norm_w, wg, wu, wd)


def _cm_in_kernel(x_ref, sh_ref, sc_ref, g_ref, win_ref, slg_ref, slb_ref, sw_ref, sb_ref,
                  gout_ref, bout_ref, *, tm, sub):
    gc = SGU_CH // SGU_GROUPS
    n_sub = tm // sub
    zs = {}

    def project(r):
        x = x_ref[r * sub:(r + 1) * sub, :]
        h = (_rms(x, g_ref[...]) * (1.0 + sc_ref[...]) + sh_ref[...]).astype(BF16)
        zs[r] = jnp.dot(h, win_ref[...], preferred_element_type=F32)

    project(0)
    for r in range(n_sub):
        if r + 1 < n_sub:
            project(r + 1)
        z = zs.pop(r)
        a_val = z[:, :CONV_CH]
        a_gate = z[:, CONV_CH:2 * CONV_CH]
        u = jax.nn.gelu(z[:, 2 * CONV_CH:2 * CONV_CH + SGU_CH])
        v = _ln(jax.nn.gelu(z[:, 2 * CONV_CH + SGU_CH:]), slg_ref[...], slb_ref[...]).astype(BF16)
        gout_ref[r * sub:(r + 1) * sub, :] = a_val * jax.nn.sigmoid(a_gate)
        for n in range(sub // CHUNK):
            rows = slice(n * CHUNK, (n + 1) * CHUNK)
            out_rows = slice(r * sub + n * CHUNK, r * sub + (n + 1) * CHUNK)
            for gi in range(SGU_GROUPS):
                cols = slice(gi * gc, (gi + 1) * gc)
                vc = jnp.dot(sw_ref[gi], v[rows, cols], preferred_element_type=F32) + sb_ref[:, gi:gi + 1]
                bout_ref[out_rows, cols] = u[rows, cols] * vc


def _cm_in(lay, x, mod_l, norm_w, w_in, sln_g, sln_b, sgu_w, sgu_bt, l, j):
    tm = 1024
    tile = pl.BlockSpec((tm, D_MODEL), lambda i: (i, 0))
    half = pl.BlockSpec((tm, CONV_CH), lambda i: (i, 0))
    n_in = 2 * CONV_CH + 2 * SGU_CH
    return pl.pallas_call(
        functools.partial(_cm_in_kernel, tm=tm, sub=256),
        grid=(lay.t // tm,),
        in_specs=[tile, _mod_spec(lay, tm, 3), _mod_spec(lay, tm, 4), _vec_spec(l * 6 + 2),
                  _resident((None, D_MODEL, n_in), (j, 0, 0)),
                  pl.BlockSpec((None, 1, SGU_CH), lambda i: (j, 0, 0)),
                  pl.BlockSpec((None, 1, SGU_CH), lambda i: (j, 0, 0)),
                  _resident((None, SGU_GROUPS, CHUNK, CHUNK), (j, 0, 0, 0)),
                  pl.BlockSpec((None, CHUNK, SGU_GROUPS), lambda i: (j, 0, 0))],
        out_specs=[half, half],
        out_shape=[jax.ShapeDtypeStruct((lay.t, CONV_CH), F32),
                   jax.ShapeDtypeStruct((lay.t, SGU_CH), F32)],
        compiler_params=_params(40),
        name="conv_gmlp_in",
    )(x, mod_l, mod_l, norm_w, w_in, sln_g, sln_b, sgu_w, sgu_bt)


def _cm_out_kernel(x_ref, gt_ref, g3_ref, gp_ref, gc_ref, gn_ref, bo_ref, cw_ref, cb_ref,
                   clg_ref, clb_ref, wout_ref, o_ref, gbuf, sbuf, abuf, *, tm, win, p_tiles, per_seq):
    i = pl.program_id(0)
    is_latent = i >= p_tiles
    local = (i - p_tiles) % per_seq
    n_win = tm // win
    span = win + 2 * HALO - SUBLANES
    rb = 32
    for w in range(n_win):
        if w == 0:
            left = jnp.where(jnp.logical_and(is_latent, local != 0), gp_ref[...], 0.0)
        else:
            left = jnp.where(is_latent, gc_ref[w * win - HALO:w * win, :], 0.0)
        if w == n_win - 1:
            right = jnp.where(jnp.logical_and(is_latent, local != per_seq - 1), gn_ref[...], 0.0)
        else:
            right = jnp.where(is_latent, gc_ref[(w + 1) * win:(w + 1) * win + HALO, :], 0.0)
        gbuf[0:HALO, :] = left
        gbuf[HALO:HALO + win, :] = gc_ref[w * win:(w + 1) * win, :]
        gbuf[HALO + win:, :] = right
        for s in range(1, SUBLANES):
            sbuf[s - 1] = gbuf[s:s + span, :]
        for r in range(win // rb):
            acc = jnp.broadcast_to(cb_ref[...], (rb // SUBLANES, SUBLANES, CONV_CH))
            for k in range(CONV_WIDTH):
                off = HALO - CONV_HALF + k
                s, base = off % SUBLANES, r * rb + off - off % SUBLANES
                src = gbuf[base:base + rb, :] if s == 0 else sbuf[s - 1, base:base + rb, :]
                acc = acc + cw_ref[k][None] * src.reshape(rb // SUBLANES, SUBLANES, CONV_CH)
            a = _ln(acc.reshape(rb, CONV_CH), clg_ref[...], clb_ref[...])
            abuf[w * win + r * rb:w * win + (r + 1) * rb, :] = (a * jax.nn.sigmoid(a)).astype(BF16)
    y = jnp.dot(abuf[...], wout_ref[:CONV_CH, :], preferred_element_type=F32)
    y = y + jnp.dot(bo_ref[...].astype(BF16), wout_ref[CONV_CH:, :], preferred_element_type=F32)
    o_ref[...] = x_ref[...] + gt_ref[...] * _rms(y, g3_ref[...])


def _cm_out(lay, x, mod_l, norm_w, g, b_out, conv_w, conv_b, cln_g, cln_b, w_out, l, j):
    win = lay.prompt_len
    tm = 2 * win
    assert lay.tp % tm == 0 and lay.dec_len % tm == 0 and win % HALO == 0
    hb = tm // HALO
    last = lay.t // HALO - 1
    tile = pl.BlockSpec((tm, D_MODEL), lambda i: (i, 0))
    half = pl.BlockSpec((tm, CONV_CH), lambda i: (i, 0))
    vec = pl.BlockSpec((None, 1, CONV_CH), lambda i: (j, 0, 0))
    return pl.pallas_call(
        functools.partial(_cm_out_kernel, tm=tm, win=win, p_tiles=lay.tp // tm, per_seq=lay.dec_len // tm),
        grid=(lay.t // tm,),
        in_specs=[tile, _mod_spec(lay, tm, 5), _vec_spec(l * 6 + 3),
                  pl.BlockSpec((HALO, CONV_CH), lambda i: (jnp.maximum(i * hb - 1, 0), 0)),
                  half,
                  pl.BlockSpec((HALO, CONV_CH), lambda i: (jnp.minimum((i + 1) * hb, last), 0)),
                  half,
                  pl.BlockSpec((None, CONV_WIDTH, SUBLANES, CONV_CH), lambda i: (j, 0, 0, 0)),
                  vec, vec, vec,
                  _resident((None, CONV_CH + SGU_CH, D_MODEL), (j, 0, 0))],
        out_specs=tile,
        out_shape=jax.ShapeDtypeStruct((lay.t, D_MODEL), F32),
        scratch_shapes=[pltpu.VMEM((win + 2 * HALO, CONV_CH), F32),
                        pltpu.VMEM((SUBLANES - 1, win + 2 * HALO - SUBLANES, CONV_CH), F32),
                        pltpu.VMEM((tm, CONV_CH), BF16)],
        compiler_params=_params(32),
        name="conv_gmlp_out",
    )(x, mod_l, norm_w, g, g, g, b_out, conv_w, conv_b, cln_g, cln_b, w_out)


def _rope_tables(dec_len, pad_rows):
    pos = np.arange(dec_len)
    r = (pos // GRID_W).astype(np.float64)
    c = (pos % GRID_W).astype(np.float64)
    inv = np.power(ROPE_BASE, -np.arange(0, ROPE_HALF, 2, dtype=np.float64) / ROPE_HALF)
    ang_r = r[:, None] * inv[None, :]
    ang_c = c[:, None] * inv[None, :]
    cos = np.concatenate([np.cos(ang_r)] * 2 + [np.cos(ang_c)] * 2, axis=-1)
    sin = np.concatenate([-np.sin(ang_r), np.sin(ang_r), -np.sin(ang_c), np.sin(ang_c)], axis=-1)
    cos = np.concatenate([np.ones((pad_rows, HEAD_DIM)), cos], axis=0)
    sin = np.concatenate([np.zeros((pad_rows, HEAD_DIM)), sin], axis=0)
    return (jnp.asarray(np.tile(cos, (1, 2)), F32), jnp.asarray(np.tile(sin, (1, 2)), F32))


def _rope(z, cos, sin, first_half):
    partner = jnp.where(first_half, pltpu.roll(z, 128 - ROPE_HALF // 2, 1), pltpu.roll(z, ROPE_HALF // 2, 1))
    return z * cos + partner * sin


def _qkv_kernel(x_ref, sh_ref, sc_ref, g_ref, w_ref, cos_ref, sin_ref, q_ref, k_ref, v_ref, kb_ref, vb_ref,
                *, tm, sub, p_tiles):
    lane = lax.broadcasted_iota(jnp.int32, (sub, 128), 1)
    first_half = (lane % ROPE_HALF) < (ROPE_HALF // 2)
    n_sub = tm // sub
    zs, ks, vs = {}, [], []

    def project(r):
        x = x_ref[r * sub:(r + 1) * sub, :]
        h = (_rms(x, g_ref[...]) * (1.0 + sc_ref[...]) + sh_ref[...]).astype(BF16)
        zs[r] = jnp.dot(h, w_ref[...], preferred_element_type=F32)

    project(0)
    for r in range(n_sub):
        if r + 1 < n_sub:
            project(r + 1)
        z = zs.pop(r)
        rows = slice(r * sub, (r + 1) * sub)
        cos = cos_ref[rows, :]
        sin = sin_ref[rows, :]
        for c in range(Q_DIM // 128):
            zc = _rope(z[:, c * 128:(c + 1) * 128], cos, sin, first_half)
            q_ref[rows, c * 128:(c + 1) * 128] = (zc * (ATTN_SCALE * LOG2E)).astype(BF16)
        k = jnp.concatenate([_rope(z[:, Q_DIM + c * 128:Q_DIM + (c + 1) * 128], cos, sin, first_half)
                             for c in range(KV_DIM // 128)], axis=1)
        v = z[:, Q_DIM + KV_DIM:]
        kb_ref[rows, :] = k.astype(BF16)
        vb_ref[rows, :] = v.astype(BF16)
        ks.append(k)
        vs.append(v)

    @pl.when(pl.program_id(0) < p_tiles)
    def _():
        k_ref[...] = jnp.concatenate(ks, axis=0)
        v_ref[...] = jnp.concatenate(vs, axis=0)


def _qkv(lay, x, mod_l, norm_w, w_qkv, cos_t, sin_t, l, j, tm):
    p_tiles = lay.tp // tm
    per_seq = lay.dec_len // tm
    tile = pl.BlockSpec((tm, D_MODEL), lambda i: (i, 0))
    tab = pl.BlockSpec((tm, 128), lambda i: (jnp.where(i < p_tiles, 0, 1 + (i - p_tiles) % per_seq), 0))
    kv = pl.BlockSpec((tm, KV_DIM), lambda i: (i, 0))
    kv_prompt = pl.BlockSpec((tm, KV_DIM), lambda i: (jnp.minimum(i, p_tiles - 1), 0))
    return pl.pallas_call(
        functools.partial(_qkv_kernel, tm=tm, sub=256, p_tiles=p_tiles),
        grid=(lay.t // tm,),
        in_specs=[tile, _mod_spec(lay, tm, 3), _mod_spec(lay, tm, 4), _vec_spec(l * 6 + 2),
                  _resident((None, D_MODEL, Q_DIM + 2 * KV_DIM), (j, 0, 0)), tab, tab],
        out_specs=[pl.BlockSpec((tm, Q_DIM), lambda i: (i, 0)), kv_prompt, kv_prompt, kv, kv],
        out_shape=[jax.ShapeDtypeStruct((lay.t, Q_DIM), BF16),
                   jax.ShapeDtypeStruct((lay.tp, KV_DIM), F32),
                   jax.ShapeDtypeStruct((lay.tp, KV_DIM), F32),
                   jax.ShapeDtypeStruct((lay.t, KV_DIM), BF16),
                   jax.ShapeDtypeStruct((lay.t, KV_DIM), BF16)],
        compiler_params=_params(40),
        name="qkv_rope",
    )(x, mod_l, mod_l, norm_w, w_qkv, cos_t, sin_t)


def _attend(q_ref, sink_ref, o_scr, s_scr, blocks, j, side_work):
    rowblk = lax.broadcasted_iota(jnp.int32, (GROUP * BLOCK, 1), 0) // BLOCK
    low_lanes = lax.broadcasted_iota(jnp.int32, (BLOCK, 2 * HEAD_DIM), 1) < HEAD_DIM
    n_slots = s_scr.shape[0]

    def score_pass(u, blk, kvh):
        lanes = slice(kvh * HEAD_DIM, (kvh + 1) * HEAD_DIM)
        q = jnp.concatenate(
            [q_ref[blk["rows"], (kvh * GROUP + g) * HEAD_DIM:(kvh * GROUP + g + 1) * HEAD_DIM]
             for g in range(GROUP)], axis=0)
        sink = jnp.zeros((GROUP * BLOCK, 1), F32)
        for g in range(GROUP):
            sink = jnp.where(rowblk == g, sink_ref[j, kvh * GROUP + g] * LOG2E, sink)
        cols, lo = [], 0
        for k_ref, k_rows, mask in blk["keys"]:
            s = lax.dot_general(q, k_ref[k_rows, lanes], (((1,), (1,)), ((), ())), preferred_element_type=F32)
            if mask is not None:
                s = jnp.where(mask, s, NEG_INF)
            s_scr[u % n_slots, :, lo:lo + s.shape[1]] = s
            lo += s.shape[1]
            cols += [s[:, c * BLOCK:(c + 1) * BLOCK] for c in range(s.shape[1] // BLOCK)]
        m = jnp.maximum(jnp.max(functools.reduce(jnp.maximum, cols), axis=-1, keepdims=True), sink)
        return sink, m

    def value_pass(u, blk, kvh, sink, m):
        pair = slice((kvh // 2) * 2 * HEAD_DIM, (kvh // 2 + 1) * 2 * HEAD_DIM)
        psum = jnp.zeros((GROUP * BLOCK, BLOCK), F32)
        o2 = jnp.zeros((GROUP * BLOCK, 2 * HEAD_DIM), F32)
        lo = 0
        for group in blk["values"]:
            v = [v_ref[v_rows, pair] for v_ref, v_rows in group]
            v = v[0] if len(v) == 1 else jnp.concatenate(v, axis=0)
            rows = v.shape[0]
            p = jnp.exp2(s_scr[u % n_slots, :, lo:lo + rows] - m)
            lo += rows
            for c in range(rows // BLOCK):
                psum = psum + p[:, c * BLOCK:(c + 1) * BLOCK]
            o2 = o2 + jnp.dot(p.astype(BF16), v, preferred_element_type=F32)
        denom = jnp.sum(psum, axis=-1, keepdims=True) + jnp.exp2(sink - m)
        o2 = o2 * (1.0 / denom)
        for half in range(GROUP // 2):
            a = o2[(2 * half) * BLOCK:(2 * half + 1) * BLOCK]
            b = o2[(2 * half + 1) * BLOCK:(2 * half + 2) * BLOCK]
            if kvh % 2 == 0:
                b = pltpu.roll(b, HEAD_DIM, 1)
            else:
                a = pltpu.roll(a, HEAD_DIM, 1)
            c = kvh * (GROUP // 2) + half
            o_scr[blk["rows"], c * 2 * HEAD_DIM:(c + 1) * 2 * HEAD_DIM] = jnp.where(low_lanes, a, b).astype(BF16)

    units = [(blk, kvh) for blk in blocks for kvh in range(N_KV_HEADS)]
    parked = {}
    side_work()
    for step in range(len(units) + 1):
        if step < len(units):
            parked[step] = score_pass(step, *units[step])
        if step >= 1:
            value_pass(step - 1, *units[step - 1], *parked.pop(step - 1))


def _attn_kernel(sink_ref, x_ref, gt_ref, g3_ref, q_ref, kp_ref, kc_ref, kn_ref, vp_ref, vc_ref, vn_ref,
                 ck_ref, cv_ref, wo_ref, o_ref, o_all, s_scr, *, n_steps, p_steps, per_seq, j):
    i = pl.program_id(0)
    c = jnp.minimum(i, n_steps - 1)
    f = jnp.maximum(i - 1, 0)
    is_prompt = c < p_steps
    lo, hi, both = slice(0, BLOCK), slice(BLOCK, 2 * BLOCK), slice(0, 2 * BLOCK)
    full = slice(None)
    o_scr = o_all.at[c % 2]

    @pl.when(i == 0)
    def _():
        o_all[0] = jnp.zeros(o_all.shape[1:], BF16)

    def finish():
        y = jnp.dot(o_all[f % 2], wo_ref[...], preferred_element_type=F32)
        o_ref[...] = x_ref[...] + gt_ref[...] * _rms(y, g3_ref[...])

    @pl.when(is_prompt)
    def _():
        blocks = [dict(rows=rows, keys=[(kc_ref, lo, None), (kc_ref, hi, None)],
                       values=[[(vc_ref, lo)], [(vc_ref, hi)]]) for rows in (lo, hi)]
        _attend(q_ref, sink_ref, o_scr, s_scr, blocks, j, finish)

    @pl.when(jnp.logical_not(is_prompt))
    def _():
        local = (c - p_steps) % per_seq
        row = lax.broadcasted_iota(jnp.int32, (GROUP * BLOCK, BLOCK), 0) % BLOCK
        col = lax.broadcasted_iota(jnp.int32, (GROUP * BLOCK, BLOCK), 1)
        behind, ahead = col >= row, col <= row
        first_mask = jnp.logical_and(behind, local != 0)
        last_mask = jnp.logical_and(ahead, local != per_seq - 1)
        blocks = [
            dict(rows=lo,
                 keys=[(kp_ref, full, first_mask), (kc_ref, lo, None), (kc_ref, hi, ahead), (ck_ref, full, None)],
                 values=[[(vp_ref, full), (vc_ref, both)], [(cv_ref, full)]]),
            dict(rows=hi,
                 keys=[(kc_ref, lo, behind), (kc_ref, hi, None), (kn_ref, full, last_mask), (ck_ref, full, None)],
                 values=[[(vc_ref, both), (vn_ref, full)], [(cv_ref, full)]]),
        ]
        _attend(q_ref, sink_ref, o_scr, s_scr, blocks, j, finish)


def _attention(lay, x, mod_l, norm_w, q, k, v, cache_k, cache_v, sink, w_o, l, j):
    tm = 2 * BLOCK
    assert lay.prompt_len == tm and lay.dec_len % tm == 0
    n_steps = lay.t // tm
    p_steps = lay.tp // tm
    per_seq = lay.dec_len // tm
    ct = lambda i: jnp.minimum(i, n_steps - 1)
    ft = lambda i: jnp.maximum(i - 1, 0)

    def prev_idx(i):
        c = ct(i)
        local = (c - p_steps) % per_seq
        return jnp.where(jnp.logical_or(c < p_steps, local == 0), 2 * c, 2 * c - 1)

    def next_idx(i):
        c = ct(i)
        local = (c - p_steps) % per_seq
        return jnp.where(jnp.logical_or(c < p_steps, local == per_seq - 1), 2 * c + 1, 2 * c + 2)

    def cache_idx(i):
        c = ct(i)
        return jnp.where(c < p_steps, 0, (c - p_steps) // per_seq)

    out_tile = pl.BlockSpec((tm, D_MODEL), lambda i: (ft(i), 0))
    gate = pl.BlockSpec((None, 1, D_MODEL), lambda i: (lay.mod_row(ft(i), tm), 0, 5))
    kv_prev = pl.BlockSpec((BLOCK, KV_DIM), lambda i: (prev_idx(i), 0))
    kv_cur = pl.BlockSpec((tm, KV_DIM), lambda i: (ct(i), 0))
    kv_next = pl.BlockSpec((BLOCK, KV_DIM), lambda i: (next_idx(i), 0))
    past = cache_k.shape[2]
    ctx = pl.BlockSpec((None, None, past, KV_DIM), lambda i: (cache_idx(i), j, 0, 0))
    return pl.pallas_call(
        functools.partial(_attn_kernel, n_steps=n_steps, p_steps=p_steps, per_seq=per_seq, j=j),
        grid=(n_steps + 1,),
        in_specs=[pl.BlockSpec(memory_space=pltpu.SMEM),
                  out_tile, gate, _vec_spec(l * 6 + 3),
                  pl.BlockSpec((tm, Q_DIM), lambda i: (ct(i), 0)),
                  kv_prev, kv_cur, kv_next, kv_prev, kv_cur, kv_next, ctx, ctx,
                  _resident((None, Q_DIM, D_MODEL), (j, 0, 0))],
        out_specs=out_tile,
        out_shape=jax.ShapeDtypeStruct((lay.t, D_MODEL), F32),
        scratch_shapes=[pltpu.VMEM((2, tm, Q_DIM), BF16),
                        pltpu.VMEM((N_KV_HEADS, GROUP * BLOCK, 3 * BLOCK + past), F32)],
        compiler_params=_params(32),
        name="window_attention",
    )(sink, x, mod_l, norm_w, q, k, k, k, v, v, v, cache_k, cache_v, w_o)


def kernel(x_prompt, x_sample, cache_k, cache_v, c, c_ctx, w_mod, b_mod, norm_w, ffn_w_gate, ffn_w_up,
           ffn_w_down, cm_w_in, cm_conv_w, cm_conv_b, cm_conv_ln_g, cm_conv_ln_b, cm_sgu_ln_g, cm_sgu_ln_b,
           cm_sgu_w, cm_sgu_b, cm_w_out, attn_w_qkv, attn_w_o, attn_sink):
    n_prompt, prompt_len, _ = x_prompt.shape
    n_dec, dec_len, _ = x_sample.shape
    lay = _Layout(n_prompt, prompt_len, n_dec, dec_len)
    n_attn = attn_w_qkv.shape[0]
    n_cm = cm_w_in.shape[0]
    past = cache_k.shape[2]

    x = (x_prompt.reshape(lay.tp, D_MODEL), x_sample.reshape(lay.ts, D_MODEL))
    cond = jnp.concatenate([c_ctx[None, :], c, jnp.zeros((MOD_ROWS - 1 - n_dec, D_MODEL), F32)], axis=0)
    mod = _modulation(cond, w_mod, b_mod).reshape(DEPTH, MOD_ROWS, 1, N_MOD * D_MODEL)
    norm_rows = norm_w.reshape(DEPTH * 6, 1, D_MODEL)

    wg = ffn_w_gate.astype(BF16)
    wu = ffn_w_up.astype(BF16)
    wd = ffn_w_down.astype(BF16)
    w_in = cm_w_in.astype(BF16)
    w_out = cm_w_out.astype(BF16)
    sgu_w = cm_sgu_w.astype(BF16)
    sgu_bt = jnp.swapaxes(cm_sgu_b, 1, 2)
    w_qkv = attn_w_qkv.astype(BF16)
    w_o = attn_w_o.astype(BF16)
    vec_cm = lambda a: a.reshape(n_cm, 1, -1)
    conv_w = jnp.broadcast_to(cm_conv_w[:, :, None, :], (n_cm, CONV_WIDTH, SUBLANES, CONV_CH))
    ck = cache_k.reshape(n_dec, n_attn, past, KV_DIM).astype(BF16)
    cv = cache_v.reshape(n_dec, n_attn, past, KV_DIM).astype(BF16)
    qkv_tm = 1024
    cos_t, sin_t = _rope_tables(dec_len, qkv_tm)

    new_k, new_v = [], []
    for l in range(DEPTH):
        mod_l = mod[l]
        j = l // 2
        x = _ffn(lay, x, mod_l, norm_rows, wg, wu, wd, l, 0, 0)
        if l % 2 == 0:
            g, b_out = _cm_in(lay, x, mod_l, norm_rows, w_in, vec_cm(cm_sgu_ln_g), vec_cm(cm_sgu_ln_b),
                              sgu_w, sgu_bt, l, j)
            x = _cm_out(lay, x, mod_l, norm_rows, g, b_out, conv_w, vec_cm(cm_conv_b),
                        vec_cm(cm_conv_ln_g), vec_cm(cm_conv_ln_b), w_out, l, j)
        else:
            q, k, v, kb, vb = _qkv(lay, x, mod_l, norm_rows, w_qkv, cos_t, sin_t, l, j, qkv_tm)
            new_k.append(k.reshape(n_prompt, prompt_len, N_KV_HEADS, HEAD_DIM))
            new_v.append(v.reshape(n_prompt, prompt_len, N_KV_HEADS, HEAD_DIM))
            x = _attention(lay, x, mod_l, norm_rows, q, kb, vb, ck, cv, attn_sink, w_o, l, j)
        x = _ffn(lay, x, mod_l, norm_rows, wg, wu, wd, l, 1, 2, split_out=(l == DEPTH - 1))
    y_prompt, y_sample = x
    return (y_prompt.reshape(x_prompt.shape), y_sample.reshape(x_sample.shape),
            jnp.stack(new_k, axis=1), jnp.stack(new_v, axis=1))
```

```python
import functools

import numpy as np
import jax
import jax.numpy as jnp
from jax import lax
from jax.experimental import pallas as pl
from jax.experimental.pallas import tpu as pltpu

F32 = jnp.float32
BF16 = jnp.bfloat16

D_MODEL = 1024
DEPTH = 4
GRID_W = 64
N_MOD = 9
D_FF = 2816
CONV_CH = 512
CONV_WIDTH = 31
CONV_HALF = CONV_WIDTH // 2
SGU_CH = 512
SGU_GROUPS = 4
CHUNK = 128
HEAD_DIM = 64
N_HEADS = 16
N_KV_HEADS = 4
GROUP = N_HEADS // N_KV_HEADS
Q_DIM = N_HEADS * HEAD_DIM
KV_DIM = N_KV_HEADS * HEAD_DIM
BLOCK = 128
ROPE_HALF = HEAD_DIM // 2
ROPE_BASE = 10000.0
EPS = 1e-6
NEG_INF = -1e30
ATTN_SCALE = HEAD_DIM ** -0.5
LOG2E = float(np.log2(np.e))

SUBLANES = 8
MOD_ROWS = 8
HALO = 16
MIB = 1024 * 1024


def _rms(x, g):
    return x * lax.rsqrt(jnp.mean(x * x, axis=-1, keepdims=True) + EPS) * g


def _ln(x, g, b):
    mu = jnp.mean(x, axis=-1, keepdims=True)
    xc = x - mu
    var = jnp.mean(xc * xc, axis=-1, keepdims=True)
    return xc * lax.rsqrt(var + EPS) * g + b


def _resident(block_shape, index):
    return pl.BlockSpec(block_shape, lambda i: index, pipeline_mode=pl.Buffered(1))


def _params(vmem_mib):
    return pltpu.CompilerParams(dimension_semantics=("arbitrary",),
                                vmem_limit_bytes=vmem_mib * MIB)


class _Layout:
    def __init__(self, n_prompt, prompt_len, n_dec, dec_len):
        self.n_prompt, self.prompt_len = n_prompt, prompt_len
        self.n_dec, self.dec_len = n_dec, dec_len
        self.tp = n_prompt * prompt_len
        self.ts = n_dec * dec_len
        self.t = self.tp + self.ts

    def mod_row(self, i, tm):
        p_tiles = self.tp // tm
        per_seq = self.dec_len // tm
        return jnp.where(i < p_tiles, 0, 1 + (i - p_tiles) // per_seq)


def _vec_spec(row):
    return pl.BlockSpec((None, 1, D_MODEL), lambda i: (row, 0, 0))


def _mod_spec(lay, tm, col):
    return pl.BlockSpec((None, 1, D_MODEL), lambda i: (lay.mod_row(i, tm), 0, col))


def _mod_kernel(c_ref, w_ref, b_ref, o_ref):
    c = c_ref[...]
    s = (c * jax.nn.sigmoid(c)).astype(BF16)
    o_ref[...] = jnp.dot(s, w_ref[...].astype(BF16), preferred_element_type=F32) + b_ref[...]


def _modulation(cond, w_mod, b_mod):
    tn = 1024
    n = N_MOD * D_MODEL
    return pl.pallas_call(
        _mod_kernel,
        grid=(DEPTH, n // tn),
        in_specs=[pl.BlockSpec((MOD_ROWS, D_MODEL), lambda l, j: (0, 0)),
                  pl.BlockSpec((None, D_MODEL, tn), lambda l, j: (l, 0, j)),
                  pl.BlockSpec((None, 1, tn), lambda l, j: (l, 0, j))],
        out_specs=pl.BlockSpec((None, MOD_ROWS, tn), lambda l, j: (l, 0, j)),
        out_shape=jax.ShapeDtypeStruct((DEPTH, MOD_ROWS, n), F32),
        compiler_params=pltpu.CompilerParams(dimension_semantics=("arbitrary", "arbitrary"),
                                             vmem_limit_bytes=24 * MIB),
        name="modulation",
    )(cond, w_mod, b_mod.reshape(DEPTH, 1, n))


def _ffn_kernel(*refs, weight, p_tiles, split_in, split_out, sub):
    n_x = 2 if split_in else 1
    x_refs, o_refs = refs[:n_x], refs[n_x + 8:]
    sh_ref, sc_ref, gt_ref, g1_ref, g2_ref, wg_ref, wu_ref, wd_ref = refs[n_x:n_x + 8]
    is_prompt = pl.program_id(0) < p_tiles
    tm = x_refs[0].shape[0]
    n_sub = tm // sub
    xs, ab, ys, outs = {}, {}, {}, []

    def load_and_up(r):
        rows = slice(r * sub, (r + 1) * sub)
        if split_in:
            x = jnp.where(is_prompt, x_refs[0][rows, :], x_refs[1][rows, :])
        else:
            x = x_refs[0][rows, :]
        h = (_rms(x, g1_ref[...]) * (1.0 + sc_ref[...]) + sh_ref[...]).astype(BF16)
        xs[r] = x
        ab[r] = (jnp.dot(h, wg_ref[...], preferred_element_type=F32),
                 jnp.dot(h, wu_ref[...], preferred_element_type=F32))

    def gate_and_down(r):
        a, b = ab.pop(r)
        act = (a * jax.nn.sigmoid(a) * b).astype(BF16)
        ys[r] = jnp.dot(act, wd_ref[...], preferred_element_type=F32)

    def finish(r):
        outs.append(xs.pop(r) + weight * gt_ref[...] * _rms(ys.pop(r), g2_ref[...]))
        if not split_out:
            o_refs[0][r * sub:(r + 1) * sub, :] = outs[-1]

    for step in range(n_sub + 2):
        if step < n_sub:
            load_and_up(step)
        if 0 <= step - 1 < n_sub:
            gate_and_down(step - 1)
        if 0 <= step - 2 < n_sub:
            finish(step - 2)
    if split_out:
        out = jnp.concatenate(outs, axis=0)

        @pl.when(is_prompt)
        def _():
            o_refs[0][...] = out

        @pl.when(jnp.logical_not(is_prompt))
        def _():
            o_refs[1][...] = out


def _ffn(lay, xs, mod_l, norm_w, wg, wu, wd, l, s, slot, split_out=False):
    split_in = isinstance(xs, tuple)
    tm = 1024
    p_tiles = lay.tp // tm
    tile = pl.BlockSpec((tm, D_MODEL), lambda i: (i, 0))
    p_tile = pl.BlockSpec((tm, D_MODEL), lambda i: (jnp.minimum(i, p_tiles - 1), 0))
    s_tile = pl.BlockSpec((tm, D_MODEL), lambda i: (jnp.maximum(i - p_tiles, 0), 0))
    xs = xs if split_in else (xs,)
    if split_out:
        out_specs = [p_tile, s_tile]
        out_shape = [jax.ShapeDtypeStruct((lay.tp, D_MODEL), F32), jax.ShapeDtypeStruct((lay.ts, D_MODEL), F32)]
    else:
        out_specs, out_shape = tile, jax.ShapeDtypeStruct((lay.t, D_MODEL), F32)
    return pl.pallas_call(
        functools.partial(_ffn_kernel, weight=0.5, p_tiles=p_tiles, split_in=split_in, split_out=split_out,
                          sub=256),
        grid=(lay.t // tm,),
        in_specs=([p_tile, s_tile] if split_in else [tile]) + [
            _mod_spec(lay, tm, 3 * slot), _mod_spec(lay, tm, 3 * slot + 1), _mod_spec(lay, tm, 3 * slot + 2),
            _vec_spec(l * 6 + 2 * slot), _vec_spec(l * 6 + 2 * slot + 1),
            _resident((None, None, D_MODEL, D_FF), (l, s, 0, 0)),
            _resident((None, None, D_MODEL, D_FF), (l, s, 0, 0)),
            _resident((None, None, D_FF, D_MODEL), (l, s, 0, 0))],
        out_specs=out_specs,
        out_shape=out_shape,
        compiler_params=_params(56 if (split_in or split_out) else 52),
        name="swiglu",
    )(*xs, mod_l, mod_l, mod_l, norm_w, norm_w, wg, wu, wd)


def _cm_in_kernel(x_ref, sh_ref, sc_ref, g_ref, win_ref, slg_ref, slb_ref, sw_ref, sb_ref,
                  gout_ref, bout_ref, *, tm, sub):
    gc = SGU_CH // SGU_GROUPS
    n_sub = tm // sub
    zs = {}

    def project(r):
        x = x_ref[r * sub:(r + 1) * sub, :]
        h = (_rms(x, g_ref[...]) * (1.0 + sc_ref[...]) + sh_ref[...]).astype(BF16)
        zs[r] = jnp.dot(h, win_ref[...], preferred_element_type=F32)

    project(0)
    for r in range(n_sub):
        if r + 1 < n_sub:
            project(r + 1)
        z = zs.pop(r)
        a_val = z[:, :CONV_CH]
        a_gate = z[:, CONV_CH:2 * CONV_CH]
        u = jax.nn.gelu(z[:, 2 * CONV_CH:2 * CONV_CH + SGU_CH])
        v = _ln(jax.nn.gelu(z[:, 2 * CONV_CH + SGU_CH:]), slg_ref[...], slb_ref[...]).astype(BF16)
        gout_ref[r * sub:(r + 1) * sub, :] = a_val * jax.nn.sigmoid(a_gate)
        for n in range(sub // CHUNK):
            rows = slice(n * CHUNK, (n + 1) * CHUNK)
            out_rows = slice(r * sub + n * CHUNK, r * sub + (n + 1) * CHUNK)
            for gi in range(SGU_GROUPS):
                cols = slice(gi * gc, (gi + 1) * gc)
                vc = jnp.dot(sw_ref[gi], v[rows, cols], preferred_element_type=F32) + sb_ref[:, gi:gi + 1]
                bout_ref[out_rows, cols] = u[rows, cols] * vc


def _cm_in(lay, x, mod_l, norm_w, w_in, sln_g, sln_b, sgu_w, sgu_bt, l, j):
    tm = 1024
    tile = pl.BlockSpec((tm, D_MODEL), lambda i: (i, 0))
    half = pl.BlockSpec((tm, CONV_CH), lambda i: (i, 0))
    n_in = 2 * CONV_CH + 2 * SGU_CH
    return pl.pallas_call(
        functools.partial(_cm_in_kernel, tm=tm, sub=256),
        grid=(lay.t // tm,),
        in_specs=[tile, _mod_spec(lay, tm, 3), _mod_spec(lay, tm, 4), _vec_spec(l * 6 + 2),
                  _resident((None, D_MODEL, n_in), (j, 0, 0)),
                  pl.BlockSpec((None, 1, SGU_CH), lambda i: (j, 0, 0)),
                  pl.BlockSpec((None, 1, SGU_CH), lambda i: (j, 0, 0)),
                  _resident((None, SGU_GROUPS, CHUNK, CHUNK), (j, 0, 0, 0)),
                  pl.BlockSpec((None, CHUNK, SGU_GROUPS), lambda i: (j, 0, 0))],
        out_specs=[half, half],
        out_shape=[jax.ShapeDtypeStruct((lay.t, CONV_CH), F32),
                   jax.ShapeDtypeStruct((lay.t, SGU_CH), F32)],
        compiler_params=_params(40),
        name="conv_gmlp_in",
    )(x, mod_l, mod_l, norm_w, w_in, sln_g, sln_b, sgu_w, sgu_bt)


def _cm_out_kernel(x_ref, gt_ref, g3_ref, gp_ref, gc_ref, gn_ref, bo_ref, cw_ref, cb_ref,
                   clg_ref, clb_ref, wout_ref, o_ref, gbuf, sbuf, abuf, *, tm, win, p_tiles, per_seq):
    i = pl.program_id(0)
    is_latent = i >= p_tiles
    local = (i - p_tiles) % per_seq
    n_win = tm // win
    span = win + 2 * HALO - SUBLANES
    rb = 32
    for w in range(n_win):
        if w == 0:
            left = jnp.where(jnp.logical_and(is_latent, local != 0), gp_ref[...], 0.0)
        else:
            left = jnp.where(is_latent, gc_ref[w * win - HALO:w * win, :], 0.0)
        if w == n_win - 1:
            right = jnp.where(jnp.logical_and(is_latent, local != per_seq - 1), gn_ref[...], 0.0)
        else:
            right = jnp.where(is_latent, gc_ref[(w + 1) * win:(w + 1) * win + HALO, :], 0.0)
        gbuf[0:HALO, :] = left
        gbuf[HALO:HALO + win, :] = gc_ref[w * win:(w + 1) * win, :]
        gbuf[HALO + win:, :] = right
        for s in range(1, SUBLANES):
            sbuf[s - 1] = gbuf[s:s + span, :]
        for r in range(win // rb):
            acc = jnp.broadcast_to(cb_ref[...], (rb // SUBLANES, SUBLANES, CONV_CH))
            for k in range(CONV_WIDTH):
                off = HALO - CONV_HALF + k
                s, base = off % SUBLANES, r * rb + off - off % SUBLANES
                src = gbuf[base:base + rb, :] if s == 0 else sbuf[s - 1, base:base + rb, :]
                acc = acc + cw_ref[k][None] * src.reshape(rb // SUBLANES, SUBLANES, CONV_CH)
            a = _ln(acc.reshape(rb, CONV_CH), clg_ref[...], clb_ref[...])
            abuf[w * win + r * rb:w * win + (r + 1) * rb, :] = (a * jax.nn.sigmoid(a)).astype(BF16)
    y = jnp.dot(abuf[...], wout_ref[:CONV_CH, :], preferred_element_type=F32)
    y = y + jnp.dot(bo_ref[...].astype(BF16), wout_ref[CONV_CH:, :], preferred_element_type=F32)
    o_ref[...] = x_ref[...] + gt_ref[...] * _rms(y, g3_ref[...])


def _cm_out(lay, x, mod_l, norm_w, g, b_out, conv_w, conv_b, cln_g, cln_b, w_out, l, j):
    win = lay.prompt_len
    tm = 2 * win
    assert lay.tp % tm == 0 and lay.dec_len % tm == 0 and win % HALO == 0
    hb = tm // HALO
    last = lay.t // HALO - 1
    tile = pl.BlockSpec((tm, D_MODEL), lambda i: (i, 0))
    half = pl.BlockSpec((tm, CONV_CH), lambda i: (i, 0))
    vec = pl.BlockSpec((None, 1, CONV_CH), lambda i: (j, 0, 0))
    return pl.pallas_call(
        functools.partial(_cm_out_kernel, tm=tm, win=win, p_tiles=lay.tp // tm, per_seq=lay.dec_len // tm),
        grid=(lay.t // tm,),
        in_specs=[tile, _mod_spec(lay, tm, 5), _vec_spec(l * 6 + 3),
                  pl.BlockSpec((HALO, CONV_CH), lambda i: (jnp.maximum(i * hb - 1, 0), 0)),
                  half,
                  pl.BlockSpec((HALO, CONV_CH), lambda i: (jnp.minimum((i + 1) * hb, last), 0)),
                  half,
                  pl.BlockSpec((None, CONV_WIDTH, SUBLANES, CONV_CH), lambda i: (j, 0, 0, 0)),
                  vec, vec, vec,
                  _resident((None, CONV_CH + SGU_CH, D_MODEL), (j, 0, 0))],
        out_specs=tile,
        out_shape=jax.ShapeDtypeStruct((lay.t, D_MODEL), F32),
        scratch_shapes=[pltpu.VMEM((win + 2 * HALO, CONV_CH), F32),
                        pltpu.VMEM((SUBLANES - 1, win + 2 * HALO - SUBLANES, CONV_CH), F32),
                        pltpu.VMEM((tm, CONV_CH), BF16)],
        compiler_params=_params(32),
        name="conv_gmlp_out",
    )(x, mod_l, norm_w, g, g, g, b_out, conv_w, conv_b, cln_g, cln_b, w_out)


def _rope_tables(dec_len, pad_rows):
    pos = np.arange(dec_len)
    r = (pos // GRID_W).astype(np.float64)
    c = (pos % GRID_W).astype(np.float64)
    inv = np.power(ROPE_BASE, -np.arange(0, ROPE_HALF, 2, dtype=np.float64) / ROPE_HALF)
    ang_r = r[:, None] * inv[None, :]
    ang_c = c[:, None] * inv[None, :]
    cos = np.concatenate([np.cos(ang_r)] * 2 + [np.cos(ang_c)] * 2, axis=-1)
    sin = np.concatenate([-np.sin(ang_r), np.sin(ang_r), -np.sin(ang_c), np.sin(ang_c)], axis=-1)
    cos = np.concatenate([np.ones((pad_rows, HEAD_DIM)), cos], axis=0)
    sin = np.concatenate([np.zeros((pad_rows, HEAD_DIM)), sin], axis=0)
    return (jnp.asarray(np.tile(cos, (1, 2)), F32), jnp.asarray(np.tile(sin, (1, 2)), F32))


def _rope(z, cos, sin, first_half):
    partner = jnp.where(first_half, pltpu.roll(z, 128 - ROPE_HALF // 2, 1), pltpu.roll(z, ROPE_HALF // 2, 1))
    return z * cos + partner * sin


def _qkv_kernel(x_ref, sh_ref, sc_ref, g_ref, w_ref, cos_ref, sin_ref, q_ref, k_ref, v_ref, kb_ref, vb_ref,
                *, tm, sub, p_tiles):
    lane = lax.broadcasted_iota(jnp.int32, (sub, 128), 1)
    first_half = (lane % ROPE_HALF) < (ROPE_HALF // 2)
    n_sub = tm // sub
    zs, ks, vs = {}, [], []

    def project(r):
        x = x_ref[r * sub:(r + 1) * sub, :]
        h = (_rms(x, g_ref[...]) * (1.0 + sc_ref[...]) + sh_ref[...]).astype(BF16)
        zs[r] = jnp.dot(h, w_ref[...], preferred_element_type=F32)

    project(0)
    for r in range(n_sub):
        if r + 1 < n_sub:
            project(r + 1)
        z = zs.pop(r)
        rows = slice(r * sub, (r + 1) * sub)
        cos = cos_ref[rows, :]
        sin = sin_ref[rows, :]
        for c in range(Q_DIM // 128):
            zc = _rope(z[:, c * 128:(c + 1) * 128], cos, sin, first_half)
            q_ref[rows, c * 128:(c + 1) * 128] = (zc * (ATTN_SCALE * LOG2E)).astype(BF16)
        k = jnp.concatenate([_rope(z[:, Q_DIM + c * 128:Q_DIM + (c + 1) * 128], cos, sin, first_half)
                             for c in range(KV_DIM // 128)], axis=1)
        v = z[:, Q_DIM + KV_DIM:]
        kb_ref[rows, :] = k.astype(BF16)
        vb_ref[rows, :] = v.astype(BF16)
        ks.append(k)
        vs.append(v)

    @pl.when(pl.program_id(0) < p_tiles)
    def _():
        k_ref[...] = jnp.concatenate(ks, axis=0)
        v_ref[...] = jnp.concatenate(vs, axis=0)


def _qkv(lay, x, mod_l, norm_w, w_qkv, cos_t, sin_t, l, j, tm):
    p_tiles = lay.tp // tm
    per_seq = lay.dec_len // tm
    tile = pl.BlockSpec((tm, D_MODEL), lambda i: (i, 0))
    tab = pl.BlockSpec((tm, 128), lambda i: (jnp.where(i < p_tiles, 0, 1 + (i - p_tiles) % per_seq), 0))
    kv = pl.BlockSpec((tm, KV_DIM), lambda i: (i, 0))
    kv_prompt = pl.BlockSpec((tm, KV_DIM), lambda i: (jnp.minimum(i, p_tiles - 1), 0))
    return pl.pallas_call(
        functools.partial(_qkv_kernel, tm=tm, sub=256, p_tiles=p_tiles),
        grid=(lay.t // tm,),
        in_specs=[tile, _mod_spec(lay, tm, 3), _mod_spec(lay, tm, 4), _vec_spec(l * 6 + 2),
                  _resident((None, D_MODEL, Q_DIM + 2 * KV_DIM), (j, 0, 0)), tab, tab],
        out_specs=[pl.BlockSpec((tm, Q_DIM), lambda i: (i, 0)), kv_prompt, kv_prompt, kv, kv],
        out_shape=[jax.ShapeDtypeStruct((lay.t, Q_DIM), BF16),
                   jax.ShapeDtypeStruct((lay.tp, KV_DIM), F32),
                   jax.ShapeDtypeStruct((lay.tp, KV_DIM), F32),
                   jax.ShapeDtypeStruct((lay.t, KV_DIM), BF16),
                   jax.ShapeDtypeStruct((lay.t, KV_DIM), BF16)],
        compiler_params=_params(40),
        name="qkv_rope",
    )(x, mod_l, mod_l, norm_w, w_qkv, cos_t, sin_t)


def _attend(q_ref, sink_ref, o_scr, s_scr, blocks, j, side_work):
    rowblk = lax.broadcasted_iota(jnp.int32, (GROUP * BLOCK, 1), 0) // BLOCK
    low_lanes = lax.broadcasted_iota(jnp.int32, (BLOCK, 2 * HEAD_DIM), 1) < HEAD_DIM
    n_slots = s_scr.shape[0]

    def score_pass(u, blk, kvh):
        lanes = slice(kvh * HEAD_DIM, (kvh + 1) * HEAD_DIM)
        q = jnp.concatenate(
            [q_ref[blk["rows"], (kvh * GROUP + g) * HEAD_DIM:(kvh * GROUP + g + 1) * HEAD_DIM]
             for g in range(GROUP)], axis=0)
        sink = jnp.zeros((GROUP * BLOCK, 1), F32)
        for g in range(GROUP):
            sink = jnp.where(rowblk == g, sink_ref[j, kvh * GROUP + g] * LOG2E, sink)
        cols, lo = [], 0
        for k_ref, k_rows, mask in blk["keys"]:
            s = lax.dot_general(q, k_ref[k_rows, lanes], (((1,), (1,)), ((), ())), preferred_element_type=F32)
            if mask is not None:
                s = jnp.where(mask, s, NEG_INF)
            s_scr[u % n_slots, :, lo:lo + s.shape[1]] = s
            lo += s.shape[1]
            cols += [s[:, c * BLOCK:(c + 1) * BLOCK] for c in range(s.shape[1] // BLOCK)]
        m = jnp.maximum(jnp.max(functools.reduce(jnp.maximum, cols), axis=-1, keepdims=True), sink)
        return sink, m

    def value_pass(u, blk, kvh, sink, m):
        pair = slice((kvh // 2) * 2 * HEAD_DIM, (kvh // 2 + 1) * 2 * HEAD_DIM)
        o2 = jnp.zeros((GROUP * BLOCK, 2 * HEAD_DIM), F32)
        lo = 0
        for group in blk["values"]:
            v = [v_ref[v_rows, pair] for v_ref, v_rows in group]
            v = v[0] if len(v) == 1 else jnp.concatenate(v, axis=0)
            rows = v.shape[0]
            lane = lax.broadcasted_iota(jnp.int32, v.shape, 1)
            v = jnp.where((lane < HEAD_DIM) == (kvh % 2 == 0), v, jnp.ones_like(v))
            p = jnp.exp2(s_scr[u % n_slots, :, lo:lo + rows] - m)
            lo += rows
            o2 = o2 + jnp.dot(p.astype(BF16), v, preferred_element_type=F32)
        e = jnp.broadcast_to(jnp.exp2(sink - m), o2.shape)
        for half in range(GROUP // 2):
            ra, rb = slice((2 * half) * BLOCK, (2 * half + 1) * BLOCK), slice((2 * half + 1) * BLOCK,
                                                                            (2 * half + 2) * BLOCK)
            a, b = o2[ra], o2[rb]
            a_sw, b_sw = pltpu.roll(a, HEAD_DIM, 1), pltpu.roll(b, HEAD_DIM, 1)
            if kvh % 2 == 0:
                outs, sums = jnp.where(low_lanes, a, b_sw), jnp.where(low_lanes, a_sw, b)
            else:
                outs, sums = jnp.where(low_lanes, a_sw, b), jnp.where(low_lanes, a, b_sw)
            denom = sums + jnp.where(low_lanes, e[ra], e[rb])
            c = kvh * (GROUP // 2) + half
            o_scr[blk["rows"], c * 2 * HEAD_DIM:(c + 1) * 2 * HEAD_DIM] = (outs * (1.0 / denom)).astype(BF16)

    units = [(blk, kvh) for blk in blocks for kvh in range(N_KV_HEADS)]
    parked = {}
    side_work()
    for step in range(len(units) + 1):
        if step < len(units):
            parked[step] = score_pass(step, *units[step])
        if step >= 1:
            value_pass(step - 1, *units[step - 1], *parked.pop(step - 1))


def _attn_kernel(sink_ref, x_ref, gt_ref, g3_ref, q_ref, kp_ref, kc_ref, kn_ref, vp_ref, vc_ref, vn_ref,
                 ck_ref, cv_ref, wo_ref, o_ref, o_all, s_scr, *, n_steps, p_steps, per_seq, j):
    i = pl.program_id(0)
    c = jnp.minimum(i, n_steps - 1)
    f = jnp.maximum(i - 1, 0)
    is_prompt = c < p_steps
    lo, hi, both = slice(0, BLOCK), slice(BLOCK, 2 * BLOCK), slice(0, 2 * BLOCK)
    full = slice(None)
    o_scr = o_all.at[c % 2]

    @pl.when(i == 0)
    def _():
        o_all[0] = jnp.zeros(o_all.shape[1:], BF16)

    def finish():
        y = jnp.dot(o_all[f % 2], wo_ref[...], preferred_element_type=F32)
        o_ref[...] = x_ref[...] + gt_ref[...] * _rms(y, g3_ref[...])

    @pl.when(is_prompt)
    def _():
        blocks = [dict(rows=rows, keys=[(kc_ref, lo, None), (kc_ref, hi, None)],
                       values=[[(vc_ref, lo)], [(vc_ref, hi)]]) for rows in (lo, hi)]
        _attend(q_ref, sink_ref, o_scr, s_scr, blocks, j, finish)

    @pl.when(jnp.logical_not(is_prompt))
    def _():
        local = (c - p_steps) % per_seq
        row = lax.broadcasted_iota(jnp.int32, (GROUP * BLOCK, BLOCK), 0) % BLOCK
        col = lax.broadcasted_iota(jnp.int32, (GROUP * BLOCK, BLOCK), 1)
        behind, ahead = col >= row, col <= row
        first_mask = jnp.logical_and(behind, local != 0)
        last_mask = jnp.logical_and(ahead, local != per_seq - 1)
        blocks = [
            dict(rows=lo,
                 keys=[(kp_ref, full, first_mask), (kc_ref, lo, None), (kc_ref, hi, ahead), (ck_ref, full, None)],
                 values=[[(vp_ref, full), (vc_ref, both)], [(cv_ref, full)]]),
            dict(rows=hi,
                 keys=[(kc_ref, lo, behind), (kc_ref, hi, None), (kn_ref, full, last_mask), (ck_ref, full, None)],
                 values=[[(vc_ref, both), (vn_ref, full)], [(cv_ref, full)]]),
        ]
        _attend(q_ref, sink_ref, o_scr, s_scr, blocks, j, finish)


def _attention(lay, x, mod_l, norm_w, q, k, v, cache_k, cache_v, sink, w_o, l, j):
    tm = 2 * BLOCK
    assert lay.prompt_len == tm and lay.dec_len % tm == 0
    n_steps = lay.t // tm
    p_steps = lay.tp // tm
    per_seq = lay.dec_len // tm
    ct = lambda i: jnp.minimum(i, n_steps - 1)
    ft = lambda i: jnp.maximum(i - 1, 0)

    def prev_idx(i):
        c = ct(i)
        local = (c - p_steps) % per_seq
        return jnp.where(jnp.logical_or(c < p_steps, local == 0), 2 * c, 2 * c - 1)

    def next_idx(i):
        c = ct(i)
        local = (c - p_steps) % per_seq
        return jnp.where(jnp.logical_or(c < p_steps, local == per_seq - 1), 2 * c + 1, 2 * c + 2)

    def cache_idx(i):
        c = ct(i)
        return jnp.where(c < p_steps, 0, (c - p_steps) // per_seq)

    out_tile = pl.BlockSpec((tm, D_MODEL), lambda i: (ft(i), 0))
    gate = pl.BlockSpec((None, 1, D_MODEL), lambda i: (lay.mod_row(ft(i), tm), 0, 5))
    kv_prev = pl.BlockSpec((BLOCK, KV_DIM), lambda i: (prev_idx(i), 0))
    kv_cur = pl.BlockSpec((tm, KV_DIM), lambda i: (ct(i), 0))
    kv_next = pl.BlockSpec((BLOCK, KV_DIM), lambda i: (next_idx(i), 0))
    past = cache_k.shape[2]
    ctx = pl.BlockSpec((None, None, past, KV_DIM), lambda i: (cache_idx(i), j, 0, 0))
    return pl.pallas_call(
        functools.partial(_attn_kernel, n_steps=n_steps, p_steps=p_steps, per_seq=per_seq, j=j),
        grid=(n_steps + 1,),
        in_specs=[pl.BlockSpec(memory_space=pltpu.SMEM),
                  out_tile, gate, _vec_spec(l * 6 + 3),
                  pl.BlockSpec((tm, Q_DIM), lambda i: (ct(i), 0)),
                  kv_prev, kv_cur, kv_next, kv_prev, kv_cur, kv_next, ctx, ctx,
                  _resident((None, Q_DIM, D_MODEL), (j, 0, 0))],
        out_specs=out_tile,
        out_shape=jax.ShapeDtypeStruct((lay.t, D_MODEL), F32),
        scratch_shapes=[pltpu.VMEM((2, tm, Q_DIM), BF16),
                        pltpu.VMEM((N_KV_HEADS, GROUP * BLOCK, 3 * BLOCK + past), F32)],
        compiler_params=_params(32),
        name="window_attention",
    )(sink, x, mod_l, norm_w, q, k, k, k, v, v, v, cache_k, cache_v, w_o)


def kernel(x_prompt, x_sample, cache_k, cache_v, c, c_ctx, w_mod, b_mod, norm_w, ffn_w_gate, ffn_w_up,
           ffn_w_down, cm_w_in, cm_conv_w, cm_conv_b, cm_conv_ln_g, cm_conv_ln_b, cm_sgu_ln_g, cm_sgu_ln_b,
           cm_sgu_w, cm_sgu_b, cm_w_out, attn_w_qkv, attn_w_o, attn_sink):
    n_prompt, prompt_len, _ = x_prompt.shape
    n_dec, dec_len, _ = x_sample.shape
    lay = _Layout(n_prompt, prompt_len, n_dec, dec_len)
    n_attn = attn_w_qkv.shape[0]
    n_cm = cm_w_in.shape[0]
    past = cache_k.shape[2]

    x = (x_prompt.reshape(lay.tp, D_MODEL), x_sample.reshape(lay.ts, D_MODEL))
    cond = jnp.concatenate([c_ctx[None, :], c, jnp.zeros((MOD_ROWS - 1 - n_dec, D_MODEL), F32)], axis=0)
    mod = _modulation(cond, w_mod, b_mod).reshape(DEPTH, MOD_ROWS, 1, N_MOD * D_MODEL)
    norm_rows = norm_w.reshape(DEPTH * 6, 1, D_MODEL)

    wg = ffn_w_gate.astype(BF16)
    wu = ffn_w_up.astype(BF16)
    wd = ffn_w_down.astype(BF16)
    w_in = cm_w_in.astype(BF16)
    w_out = cm_w_out.astype(BF16)
    sgu_w = cm_sgu_w.astype(BF16)
    sgu_bt = jnp.swapaxes(cm_sgu_b, 1, 2)
    w_qkv = attn_w_qkv.astype(BF16)
    w_o = attn_w_o.astype(BF16)
    vec_cm = lambda a: a.reshape(n_cm, 1, -1)
    conv_w = jnp.broadcast_to(cm_conv_w[:, :, None, :], (n_cm, CONV_WIDTH, SUBLANES, CONV_CH))
    ck = cache_k.reshape(n_dec, n_attn, past, KV_DIM).astype(BF16)
    cv = cache_v.reshape(n_dec, n_attn, past, KV_DIM).astype(BF16)
    qkv_tm = 1024
    cos_t, sin_t = _rope_tables(dec_len, qkv_tm)

    new_k, new_v = [], []
    for l in range(DEPTH):
        mod_l = mod[l]
        j = l // 2
        x = _ffn(lay, x, mod_l, norm_rows, wg, wu, wd, l, 0, 0)
        if l % 2 == 0:
            g, b_out = _cm_in(lay, x, mod_l, norm_rows, w_in, vec_cm(cm_sgu_ln_g), vec_cm(cm_sgu_ln_b),
                              sgu_w, sgu_bt, l, j)
            x = _cm_out(lay, x, mod_l, norm_rows, g, b_out, conv_w, vec_cm(cm_conv_b),
                        vec_cm(cm_conv_ln_g), vec_cm(cm_conv_ln_b), w_out, l, j)
        else:
            q, k, v, kb, vb = _qkv(lay, x, mod_l, norm_rows, w_qkv, cos_t, sin_t, l, j, qkv_tm)
            new_k.append(k.reshape(n_prompt, prompt_len, N_KV_HEADS, HEAD_DIM))
            new_v.append(v.reshape(n_prompt, prompt_len, N_KV_HEADS, HEAD_DIM))
            x = _attention(lay, x, mod_l, norm_rows, q, kb, vb, ck, cv, attn_sink, w_o, l, j)
        x = _ffn(lay, x, mod_l, norm_rows, wg, wu, wd, l, 1, 2, split_out=(l == DEPTH - 1))
    y_prompt, y_sample = x
    return (y_prompt.reshape(x_prompt.shape), y_sample.reshape(x_sample.shape),
            jnp.stack(new_k, axis=1), jnp.stack(new_v, axis=1))
```

```python
import functools

import numpy as np
import jax
import jax.numpy as jnp
from jax import lax
from jax.experimental import pallas as pl
from jax.experimental.pallas import tpu as pltpu

F32 = jnp.float32
BF16 = jnp.bfloat16

D_MODEL = 1024
DEPTH = 4
GRID_W = 64
N_MOD = 9
D_FF = 2816
CONV_CH = 512
CONV_WIDTH = 31
CONV_HALF = CONV_WIDTH // 2
SGU_CH = 512
SGU_GROUPS = 4
CHUNK = 128
HEAD_DIM = 64
N_HEADS = 16
N_KV_HEADS = 4
GROUP = N_HEADS // N_KV_HEADS
Q_DIM = N_HEADS * HEAD_DIM
KV_DIM = N_KV_HEADS * HEAD_DIM
BLOCK = 128
ROPE_HALF = HEAD_DIM // 2
ROPE_BASE = 10000.0
EPS = 1e-6
NEG_INF = -1e30
ATTN_SCALE = HEAD_DIM ** -0.5
LOG2E = float(np.log2(np.e))

SUBLANES = 8
MOD_ROWS = 8
HALO = 16
MIB = 1024 * 1024


def _rms(x, g):
    return x * lax.rsqrt(jnp.mean(x * x, axis=-1, keepdims=True) + EPS) * g


def _ln(x, g, b):
    mu = jnp.mean(x, axis=-1, keepdims=True)
    xc = x - mu
    var = jnp.mean(xc * xc, axis=-1, keepdims=True)
    return xc * lax.rsqrt(var + EPS) * g + b


def _resident(block_shape, index):
    return pl.BlockSpec(block_shape, lambda i: index, pipeline_mode=pl.Buffered(1))


def _params(vmem_mib):
    return pltpu.CompilerParams(dimension_semantics=("arbitrary",),
                                vmem_limit_bytes=vmem_mib * MIB)


class _Layout:
    def __init__(self, n_prompt, prompt_len, n_dec, dec_len):
        self.n_prompt, self.prompt_len = n_prompt, prompt_len
        self.n_dec, self.dec_len = n_dec, dec_len
        self.tp = n_prompt * prompt_len
        self.ts = n_dec * dec_len
        self.t = self.tp + self.ts

    def mod_row(self, i, tm):
        p_tiles = self.tp // tm
        per_seq = self.dec_len // tm
        return jnp.where(i < p_tiles, 0, 1 + (i - p_tiles) // per_seq)


def _vec_spec(row):
    return pl.BlockSpec((None, 1, D_MODEL), lambda i: (row, 0, 0))


def _mod_spec(lay, tm, col):
    return pl.BlockSpec((None, 1, D_MODEL), lambda i: (lay.mod_row(i, tm), 0, col))


def _mod_kernel(c_ref, w_ref, b_ref, o_ref):
    c = c_ref[...]
    s = (c * jax.nn.sigmoid(c)).astype(BF16)
    o_ref[...] = jnp.dot(s, w_ref[...].astype(BF16), preferred_element_type=F32) + b_ref[...]


def _modulation(cond, w_mod, b_mod):
    tn = 1024
    n = N_MOD * D_MODEL
    return pl.pallas_call(
        _mod_kernel,
        grid=(DEPTH, n // tn),
        in_specs=[pl.BlockSpec((MOD_ROWS, D_MODEL), lambda l, j: (0, 0)),
                  pl.BlockSpec((None, D_MODEL, tn), lambda l, j: (l, 0, j)),
                  pl.BlockSpec((None, 1, tn), lambda l, j: (l, 0, j))],
        out_specs=pl.BlockSpec((None, MOD_ROWS, tn), lambda l, j: (l, 0, j)),
        out_shape=jax.ShapeDtypeStruct((DEPTH, MOD_ROWS, n), F32),
        compiler_params=pltpu.CompilerParams(dimension_semantics=("arbitrary", "arbitrary"),
                                             vmem_limit_bytes=24 * MIB),
        name="modulation",
    )(cond, w_mod, b_mod.reshape(DEPTH, 1, n))


def _ffn_kernel(*refs, weight, p_tiles, split_in, split_out, sub):
    n_x = 2 if split_in else 1
    x_refs, o_refs = refs[:n_x], refs[n_x + 8:]
    sh_ref, sc_ref, gt_ref, g1_ref, g2_ref, wg_ref, wu_ref, wd_ref = refs[n_x:n_x + 8]
    is_prompt = pl.program_id(0) < p_tiles
    tm = x_refs[0].shape[0]
    n_sub = tm // sub
    xs, ab, ys, outs = {}, {}, {}, []

    def load_and_up(r):
        rows = slice(r * sub, (r + 1) * sub)
        if split_in:
            x = jnp.where(is_prompt, x_refs[0][rows, :], x_refs[1][rows, :])
        else:
            x = x_refs[0][rows, :]
        h = (_rms(x, g1_ref[...]) * (1.0 + sc_ref[...]) + sh_ref[...]).astype(BF16)
        xs[r] = x
        ab[r] = (jnp.dot(h, wg_ref[...], preferred_element_type=F32),
                 jnp.dot(h, wu_ref[...], preferred_element_type=F32))

    def gate_and_down(r):
        a, b = ab.pop(r)
        act = (a * jax.nn.sigmoid(a) * b).astype(BF16)
        ys[r] = jnp.dot(act, wd_ref[...], preferred_element_type=F32)

    def finish(r):
        outs.append(xs.pop(r) + weight * gt_ref[...] * _rms(ys.pop(r), g2_ref[...]))
        if not split_out:
            o_refs[0][r * sub:(r + 1) * sub, :] = outs[-1]

    for step in range(n_sub + 2):
        if step < n_sub:
            load_and_up(step)
        if 0 <= step - 1 < n_sub:
            gate_and_down(step - 1)
        if 0 <= step - 2 < n_sub:
            finish(step - 2)
    if split_out:
        out = jnp.concatenate(outs, axis=0)

        @pl.when(is_prompt)
        def _():
            o_refs[0][...] = out

        @pl.when(jnp.logical_not(is_prompt))
        def _():
            o_refs[1][...] = out


def _ffn(lay, xs, mod_l, norm_w, wg, wu, wd, l, s, slot, split_out=False):
    split_in = isinstance(xs, tuple)
    tm = 1024
    p_tiles = lay.tp // tm
    tile = pl.BlockSpec((tm, D_MODEL), lambda i: (i, 0))
    p_tile = pl.BlockSpec((tm, D_MODEL), lambda i: (jnp.minimum(i, p_tiles - 1), 0))
    s_tile = pl.BlockSpec((tm, D_MODEL), lambda i: (jnp.maximum(i - p_tiles, 0), 0))
    xs = xs if split_in else (xs,)
    if split_out:
        out_specs = [p_tile, s_tile]
        out_shape = [jax.ShapeDtypeStruct((lay.tp, D_MODEL), F32), jax.ShapeDtypeStruct((lay.ts, D_MODEL), F32)]
    else:
        out_specs, out_shape = tile, jax.ShapeDtypeStruct((lay.t, D_MODEL), F32)
    return pl.pallas_call(
        functools.partial(_ffn_kernel, weight=0.5, p_tiles=p_tiles, split_in=split_in, split_out=split_out,
                          sub=256),
        grid=(lay.t // tm,),
        in_specs=([p_tile, s_tile] if split_in else [tile]) + [
            _mod_spec(lay, tm, 3 * slot), _mod_spec(lay, tm, 3 * slot + 1), _mod_spec(lay, tm, 3 * slot + 2),
            _vec_spec(l * 6 + 2 * slot), _vec_spec(l * 6 + 2 * slot + 1),
            _resident((None, None, D_MODEL, D_FF), (l, s, 0, 0)),
            _resident((None, None, D_MODEL, D_FF), (l, s, 0, 0)),
            _resident((None, None, D_FF, D_MODEL), (l, s, 0, 0))],
        out_specs=out_specs,
        out_shape=out_shape,
        compiler_params=_params(56 if (split_in or split_out) else 52),
        name="swiglu",
    )(*xs, mod_l, mod_l, mod_l, norm_w, norm_w, wg, wu, wd)


def _cm_in_kernel(x_ref, sh_ref, sc_ref, g_ref, win_ref, slg_ref, slb_ref, sw_ref, sb_ref,
                  gout_ref, bout_ref, *, tm, sub):
    gc = SGU_CH // SGU_GROUPS
    n_sub = tm // sub
    zs = {}

    def project(r):
        x = x_ref[r * sub:(r + 1) * sub, :]
        h = (_rms(x, g_ref[...]) * (1.0 + sc_ref[...]) + sh_ref[...]).astype(BF16)
        zs[r] = jnp.dot(h, win_ref[...], preferred_element_type=F32)

    project(0)
    for r in range(n_sub):
        if r + 1 < n_sub:
            project(r + 1)
        z = zs.pop(r)
        a_val = z[:, :CONV_CH]
        a_gate = z[:, CONV_CH:2 * CONV_CH]
        u = jax.nn.gelu(z[:, 2 * CONV_CH:2 * CONV_CH + SGU_CH])
        v = _ln(jax.nn.gelu(z[:, 2 * CONV_CH + SGU_CH:]), slg_ref[...], slb_ref[...]).astype(BF16)
        gout_ref[r * sub:(r + 1) * sub, :] = a_val * jax.nn.sigmoid(a_gate)
        for n in range(sub // CHUNK):
            rows = slice(n * CHUNK, (n + 1) * CHUNK)
            out_rows = slice(r * sub + n * CHUNK, r * sub + (n + 1) * CHUNK)
            for gi in range(SGU_GROUPS):
                cols = slice(gi * gc, (gi + 1) * gc)
                vc = jnp.dot(sw_ref[gi], v[rows, cols], preferred_element_type=F32) + sb_ref[:, gi:gi + 1]
                bout_ref[out_rows, cols] = u[rows, cols] * vc


def _cm_in(lay, x, mod_l, norm_w, w_in, sln_g, sln_b, sgu_w, sgu_bt, l, j):
    tm = 1024
    tile = pl.BlockSpec((tm, D_MODEL), lambda i: (i, 0))
    half = pl.BlockSpec((tm, CONV_CH), lambda i: (i, 0))
    n_in = 2 * CONV_CH + 2 * SGU_CH
    return pl.pallas_call(
        functools.partial(_cm_in_kernel, tm=tm, sub=256),
        grid=(lay.t // tm,),
        in_specs=[tile, _mod_spec(lay, tm, 3), _mod_spec(lay, tm, 4), _vec_spec(l * 6 + 2),
                  _resident((None, D_MODEL, n_in), (j, 0, 0)),
                  pl.BlockSpec((None, 1, SGU_CH), lambda i: (j, 0, 0)),
                  pl.BlockSpec((None, 1, SGU_CH), lambda i: (j, 0, 0)),
                  _resident((None, SGU_GROUPS, CHUNK, CHUNK), (j, 0, 0, 0)),
                  pl.BlockSpec((None, CHUNK, SGU_GROUPS), lambda i: (j, 0, 0))],
        out_specs=[half, half],
        out_shape=[jax.ShapeDtypeStruct((lay.t, CONV_CH), F32),
                   jax.ShapeDtypeStruct((lay.t, SGU_CH), F32)],
        compiler_params=_params(40),
        name="conv_gmlp_in",
    )(x, mod_l, mod_l, norm_w, w_in, sln_g, sln_b, sgu_w, sgu_bt)


def _cm_out_kernel(x_ref, gt_ref, g3_ref, gp_ref, gc_ref, gn_ref, bo_ref, cw_ref, cb_ref,
                   clg_ref, clb_ref, wout_ref, o_ref, gbuf, sbuf, abuf, *, tm, win, p_tiles, per_seq):
    i = pl.program_id(0)
    is_latent = i >= p_tiles
    local = (i - p_tiles) % per_seq
    n_win = tm // win
    span = win + 2 * HALO - SUBLANES
    rb = 128
    for w in range(n_win):
        if w == 0:
            left = jnp.where(jnp.logical_and(is_latent, local != 0), gp_ref[...], 0.0)
        else:
            left = jnp.where(is_latent, gc_ref[w * win - HALO:w * win, :], 0.0)
        if w == n_win - 1:
            right = jnp.where(jnp.logical_and(is_latent, local != per_seq - 1), gn_ref[...], 0.0)
        else:
            right = jnp.where(is_latent, gc_ref[(w + 1) * win:(w + 1) * win + HALO, :], 0.0)
        gbuf[0:HALO, :] = left
        gbuf[HALO:HALO + win, :] = gc_ref[w * win:(w + 1) * win, :]
        gbuf[HALO + win:, :] = right
        for s in range(1, SUBLANES):
            sbuf[s - 1] = gbuf[s:s + span, :]
        for r in range(win // rb):
            acc = jnp.broadcast_to(cb_ref[...], (rb // SUBLANES, SUBLANES, CONV_CH))
            for k in range(CONV_WIDTH):
                off = HALO - CONV_HALF + k
                s, base = off % SUBLANES, r * rb + off - off % SUBLANES
                src = gbuf[base:base + rb, :] if s == 0 else sbuf[s - 1, base:base + rb, :]
                acc = acc + cw_ref[k][None] * src.reshape(rb // SUBLANES, SUBLANES, CONV_CH)
            a = _ln(acc.reshape(rb, CONV_CH), clg_ref[...], clb_ref[...])
            abuf[w * win + r * rb:w * win + (r + 1) * rb, :] = (a * jax.nn.sigmoid(a)).astype(BF16)
    y = jnp.dot(abuf[...], wout_ref[:CONV_CH, :], preferred_element_type=F32)
    y = y + jnp.dot(bo_ref[...].astype(BF16), wout_ref[CONV_CH:, :], preferred_element_type=F32)
    o_ref[...] = x_ref[...] + gt_ref[...] * _rms(y, g3_ref[...])


def _cm_out(lay, x, mod_l, norm_w, g, b_out, conv_w, conv_b, cln_g, cln_b, w_out, l, j):
    win = lay.prompt_len
    tm = 2 * win
    assert lay.tp % tm == 0 and lay.dec_len % tm == 0 and win % HALO == 0
    hb = tm // HALO
    last = lay.t // HALO - 1
    tile = pl.BlockSpec((tm, D_MODEL), lambda i: (i, 0))
    half = pl.BlockSpec((tm, CONV_CH), lambda i: (i, 0))
    vec = pl.BlockSpec((None, 1, CONV_CH), lambda i: (j, 0, 0))
    return pl.pallas_call(
        functools.partial(_cm_out_kernel, tm=tm, win=win, p_tiles=lay.tp // tm, per_seq=lay.dec_len // tm),
        grid=(lay.t // tm,),
        in_specs=[tile, _mod_spec(lay, tm, 5), _vec_spec(l * 6 + 3),
                  pl.BlockSpec((HALO, CONV_CH), lambda i: (jnp.maximum(i * hb - 1, 0), 0)),
                  half,
                  pl.BlockSpec((HALO, CONV_CH), lambda i: (jnp.minimum((i + 1) * hb, last), 0)),
                  half,
                  pl.BlockSpec((None, CONV_WIDTH, SUBLANES, CONV_CH), lambda i: (j, 0, 0, 0)),
                  vec, vec, vec,
                  _resident((None, CONV_CH + SGU_CH, D_MODEL), (j, 0, 0))],
        out_specs=tile,
        out_shape=jax.ShapeDtypeStruct((lay.t, D_MODEL), F32),
        scratch_shapes=[pltpu.VMEM((win + 2 * HALO, CONV_CH), F32),
                        pltpu.VMEM((SUBLANES - 1, win + 2 * HALO - SUBLANES, CONV_CH), F32),
                        pltpu.VMEM((tm, CONV_CH), BF16)],
        compiler_params=_params(32),
        name="conv_gmlp_out",
    )(x, mod_l, norm_w, g, g, g, b_out, conv_w, conv_b, cln_g, cln_b, w_out)


def _rope_tables(dec_len, pad_rows):
    pos = np.arange(dec_len)
    r = (pos // GRID_W).astype(np.float64)
    c = (pos % GRID_W).astype(np.float64)
    inv = np.power(ROPE_BASE, -np.arange(0, ROPE_HALF, 2, dtype=np.float64) / ROPE_HALF)
    ang_r = r[:, None] * inv[None, :]
    ang_c = c[:, None] * inv[None, :]
    cos = np.concatenate([np.cos(ang_r)] * 2 + [np.cos(ang_c)] * 2, axis=-1)
    sin = np.concatenate([-np.sin(ang_r), np.sin(ang_r), -np.sin(ang_c), np.sin(ang_c)], axis=-1)
    cos = np.concatenate([np.ones((pad_rows, HEAD_DIM)), cos], axis=0)
    sin = np.concatenate([np.zeros((pad_rows, HEAD_DIM)), sin], axis=0)
    return (jnp.asarray(np.tile(cos, (1, 2)), F32), jnp.asarray(np.tile(sin, (1, 2)), F32))


def _rope(z, cos, sin, first_half):
    partner = jnp.where(first_half, pltpu.roll(z, 128 - ROPE_HALF // 2, 1), pltpu.roll(z, ROPE_HALF // 2, 1))
    return z * cos + partner * sin


def _qkv_kernel(x_ref, sh_ref, sc_ref, g_ref, w_ref, cos_ref, sin_ref, q_ref, k_ref, v_ref, kb_ref, vb_ref,
                *, tm, sub, p_tiles):
    lane = lax.broadcasted_iota(jnp.int32, (sub, 128), 1)
    first_half = (lane % ROPE_HALF) < (ROPE_HALF // 2)
    n_sub = tm // sub
    zs, ks, vs = {}, [], []

    def project(r):
        x = x_ref[r * sub:(r + 1) * sub, :]
        h = (_rms(x, g_ref[...]) * (1.0 + sc_ref[...]) + sh_ref[...]).astype(BF16)
        zs[r] = jnp.dot(h, w_ref[...], preferred_element_type=F32)

    project(0)
    for r in range(n_sub):
        if r + 1 < n_sub:
            project(r + 1)
        z = zs.pop(r)
        rows = slice(r * sub, (r + 1) * sub)
        cos = cos_ref[rows, :]
        sin = sin_ref[rows, :]
        for c in range(Q_DIM // 128):
            zc = _rope(z[:, c * 128:(c + 1) * 128], cos, sin, first_half)
            q_ref[rows, c * 128:(c + 1) * 128] = (zc * (ATTN_SCALE * LOG2E)).astype(BF16)
        k = jnp.concatenate([_rope(z[:, Q_DIM + c * 128:Q_DIM + (c + 1) * 128], cos, sin, first_half)
                             for c in range(KV_DIM // 128)], axis=1)
        v = z[:, Q_DIM + KV_DIM:]
        kb_ref[rows, :] = k.astype(BF16)
        vb_ref[rows, :] = v.astype(BF16)
        ks.append(k)
        vs.append(v)

    @pl.when(pl.program_id(0) < p_tiles)
    def _():
        k_ref[...] = jnp.concatenate(ks, axis=0)
        v_ref[...] = jnp.concatenate(vs, axis=0)


def _qkv(lay, x, mod_l, norm_w, w_qkv, cos_t, sin_t, l, j, tm):
    p_tiles = lay.tp // tm
    per_seq = lay.dec_len // tm
    tile = pl.BlockSpec((tm, D_MODEL), lambda i: (i, 0))
    tab = pl.BlockSpec((tm, 128), lambda i: (jnp.where(i < p_tiles, 0, 1 + (i - p_tiles) % per_seq), 0))
    kv = pl.BlockSpec((tm, KV_DIM), lambda i: (i, 0))
    kv_prompt = pl.BlockSpec((tm, KV_DIM), lambda i: (jnp.minimum(i, p_tiles - 1), 0))
    return pl.pallas_call(
        functools.partial(_qkv_kernel, tm=tm, sub=256, p_tiles=p_tiles),
        grid=(lay.t // tm,),
        in_specs=[tile, _mod_spec(lay, tm, 3), _mod_spec(lay, tm, 4), _vec_spec(l * 6 + 2),
                  _resident((None, D_MODEL, Q_DIM + 2 * KV_DIM), (j, 0, 0)), tab, tab],
        out_specs=[pl.BlockSpec((tm, Q_DIM), lambda i: (i, 0)), kv_prompt, kv_prompt, kv, kv],
        out_shape=[jax.ShapeDtypeStruct((lay.t, Q_DIM), BF16),
                   jax.ShapeDtypeStruct((lay.tp, KV_DIM), F32),
                   jax.ShapeDtypeStruct((lay.tp, KV_DIM), F32),
                   jax.ShapeDtypeStruct((lay.t, KV_DIM), BF16),
                   jax.ShapeDtypeStruct((lay.t, KV_DIM), BF16)],
        compiler_params=_params(40),
        name="qkv_rope",
    )(x, mod_l, mod_l, norm_w, w_qkv, cos_t, sin_t)


def _attend(q_ref, sink_ref, o_scr, s_scr, blocks, j, side_work):
    rowblk = lax.broadcasted_iota(jnp.int32, (GROUP * BLOCK, 1), 0) // BLOCK
    low_lanes = lax.broadcasted_iota(jnp.int32, (BLOCK, 2 * HEAD_DIM), 1) < HEAD_DIM
    n_slots = s_scr.shape[0]

    def score_pass(u, blk, kvh):
        lanes = slice(kvh * HEAD_DIM, (kvh + 1) * HEAD_DIM)
        q = jnp.concatenate(
            [q_ref[blk["rows"], (kvh * GROUP + g) * HEAD_DIM:(kvh * GROUP + g + 1) * HEAD_DIM]
             for g in range(GROUP)], axis=0)
        sink = jnp.zeros((GROUP * BLOCK, 1), F32)
        for g in range(GROUP):
            sink = jnp.where(rowblk == g, sink_ref[j, kvh * GROUP + g] * LOG2E, sink)
        cols, lo = [], 0
        for k_ref, k_rows, mask in blk["keys"]:
            s = lax.dot_general(q, k_ref[k_rows, lanes], (((1,), (1,)), ((), ())), preferred_element_type=F32)
            if mask is not None:
                s = jnp.where(mask, s, NEG_INF)
            s_scr[u % n_slots, :, lo:lo + s.shape[1]] = s
            lo += s.shape[1]
            cols += [s[:, c * BLOCK:(c + 1) * BLOCK] for c in range(s.shape[1] // BLOCK)]
        m = jnp.maximum(jnp.max(functools.reduce(jnp.maximum, cols), axis=-1, keepdims=True), sink)
        return sink, m

    def value_pass(u, blk, kvh, sink, m):
        pair = slice((kvh // 2) * 2 * HEAD_DIM, (kvh // 2 + 1) * 2 * HEAD_DIM)
        o2 = jnp.zeros((GROUP * BLOCK, 2 * HEAD_DIM), F32)
        lo = 0
        for group in blk["values"]:
            v = [v_ref[v_rows, pair] for v_ref, v_rows in group]
            v = v[0] if len(v) == 1 else jnp.concatenate(v, axis=0)
            rows = v.shape[0]
            lane = lax.broadcasted_iota(jnp.int32, v.shape, 1)
            v = jnp.where((lane < HEAD_DIM) == (kvh % 2 == 0), v, jnp.ones_like(v))
            p = jnp.exp2((s_scr[u % n_slots, :, lo:lo + rows] - m).astype(BF16))
            lo += rows
            o2 = o2 + jnp.dot(p.astype(BF16), v, preferred_element_type=F32)
        e = jnp.broadcast_to(jnp.exp2(sink - m), o2.shape)
        for half in range(GROUP // 2):
            ra, rb = slice((2 * half) * BLOCK, (2 * half + 1) * BLOCK), slice((2 * half + 1) * BLOCK,
                                                                            (2 * half + 2) * BLOCK)
            a, b = o2[ra], o2[rb]
            a_sw, b_sw = pltpu.roll(a, HEAD_DIM, 1), pltpu.roll(b, HEAD_DIM, 1)
            if kvh % 2 == 0:
                outs, sums = jnp.where(low_lanes, a, b_sw), jnp.where(low_lanes, a_sw, b)
            else:
                outs, sums = jnp.where(low_lanes, a_sw, b), jnp.where(low_lanes, a, b_sw)
            denom = sums + jnp.where(low_lanes, e[ra], e[rb])
            c = kvh * (GROUP // 2) + half
            o_scr[blk["rows"], c * 2 * HEAD_DIM:(c + 1) * 2 * HEAD_DIM] = (outs * (1.0 / denom)).astype(BF16)

    units = [(blk, kvh) for blk in blocks for kvh in range(N_KV_HEADS)]
    parked = {}
    side_work()
    for step in range(len(units) + 1):
        if step < len(units):
            parked[step] = score_pass(step, *units[step])
        if step >= 1:
            value_pass(step - 1, *units[step - 1], *parked.pop(step - 1))


def _attn_kernel(sink_ref, x_ref, gt_ref, g3_ref, q_ref, kp_ref, kc_ref, kn_ref, vp_ref, vc_ref, vn_ref,
                 ck_ref, cv_ref, wo_ref, o_ref, o_all, s_scr, *, n_steps, p_steps, per_seq, j):
    i = pl.program_id(0)
    c = jnp.minimum(i, n_steps - 1)
    f = jnp.maximum(i - 1, 0)
    is_prompt = c < p_steps
    lo, hi, both = slice(0, BLOCK), slice(BLOCK, 2 * BLOCK), slice(0, 2 * BLOCK)
    full = slice(None)
    o_scr = o_all.at[c % 2]

    @pl.when(i == 0)
    def _():
        o_all[0] = jnp.zeros(o_all.shape[1:], BF16)

    def finish():
        y = jnp.dot(o_all[f % 2], wo_ref[...], preferred_element_type=F32)
        o_ref[...] = x_ref[...] + gt_ref[...] * _rms(y, g3_ref[...])

    @pl.when(is_prompt)
    def _():
        blocks = [dict(rows=rows, keys=[(kc_ref, lo, None), (kc_ref, hi, None)],
                       values=[[(vc_ref, lo)], [(vc_ref, hi)]]) for rows in (lo, hi)]
        _attend(q_ref, sink_ref, o_scr, s_scr, blocks, j, finish)

    @pl.when(jnp.logical_not(is_prompt))
    def _():
        local = (c - p_steps) % per_seq
        row = lax.broadcasted_iota(jnp.int32, (GROUP * BLOCK, BLOCK), 0) % BLOCK
        col = lax.broadcasted_iota(jnp.int32, (GROUP * BLOCK, BLOCK), 1)
        behind, ahead = col >= row, col <= row
        first_mask = jnp.logical_and(behind, local != 0)
        last_mask = jnp.logical_and(ahead, local != per_seq - 1)
        blocks = [
            dict(rows=lo,
                 keys=[(kp_ref, full, first_mask), (kc_ref, lo, None), (kc_ref, hi, ahead), (ck_ref, full, None)],
                 values=[[(vp_ref, full), (vc_ref, both)], [(cv_ref, full)]]),
            dict(rows=hi,
                 keys=[(kc_ref, lo, behind), (kc_ref, hi, None), (kn_ref, full, last_mask), (ck_ref, full, None)],
                 values=[[(vc_ref, both), (vn_ref, full)], [(cv_ref, full)]]),
        ]
        _attend(q_ref, sink_ref, o_scr, s_scr, blocks, j, finish)


def _attention(lay, x, mod_l, norm_w, q, k, v, cache_k, cache_v, sink, w_o, l, j):
    tm = 2 * BLOCK
    assert lay.prompt_len == tm and lay.dec_len % tm == 0
    n_steps = lay.t // tm
    p_steps = lay.tp // tm
    per_seq = lay.dec_len // tm
    ct = lambda i: jnp.minimum(i, n_steps - 1)
    ft = lambda i: jnp.maximum(i - 1, 0)

    def prev_idx(i):
        c = ct(i)
        local = (c - p_steps) % per_seq
        return jnp.where(jnp.logical_or(c < p_steps, local == 0), 2 * c, 2 * c - 1)

    def next_idx(i):
        c = ct(i)
        local = (c - p_steps) % per_seq
        return jnp.where(jnp.logical_or(c < p_steps, local == per_seq - 1), 2 * c + 1, 2 * c + 2)

    def cache_idx(i):
        c = ct(i)
        return jnp.where(c < p_steps, 0, (c - p_steps) // per_seq)

    out_tile = pl.BlockSpec((tm, D_MODEL), lambda i: (ft(i), 0))
    gate = pl.BlockSpec((None, 1, D_MODEL), lambda i: (lay.mod_row(ft(i), tm), 0, 5))
    kv_prev = pl.BlockSpec((BLOCK, KV_DIM), lambda i: (prev_idx(i), 0))
    kv_cur = pl.BlockSpec((tm, KV_DIM), lambda i: (ct(i), 0))
    kv_next = pl.BlockSpec((BLOCK, KV_DIM), lambda i: (next_idx(i), 0))
    past = cache_k.shape[2]
    ctx = pl.BlockSpec((None, None, past, KV_DIM), lambda i: (cache_idx(i), j, 0, 0))
    return pl.pallas_call(
        functools.partial(_attn_kernel, n_steps=n_steps, p_steps=p_steps, per_seq=per_seq, j=j),
        grid=(n_steps + 1,),
        in_specs=[pl.BlockSpec(memory_space=pltpu.SMEM),
                  out_tile, gate, _vec_spec(l * 6 + 3),
                  pl.BlockSpec((tm, Q_DIM), lambda i: (ct(i), 0)),
                  kv_prev, kv_cur, kv_next, kv_prev, kv_cur, kv_next, ctx, ctx,
                  _resident((None, Q_DIM, D_MODEL), (j, 0, 0))],
        out_specs=out_tile,
        out_shape=jax.ShapeDtypeStruct((lay.t, D_MODEL), F32),
        scratch_shapes=[pltpu.VMEM((2, tm, Q_DIM), BF16),
                        pltpu.VMEM((N_KV_HEADS, GROUP * BLOCK, 3 * BLOCK + past), F32)],
        compiler_params=_params(32),
        name="window_attention",
    )(sink, x, mod_l, norm_w, q, k, k, k, v, v, v, cache_k, cache_v, w_o)


def kernel(x_prompt, x_sample, cache_k, cache_v, c, c_ctx, w_mod, b_mod, norm_w, ffn_w_gate, ffn_w_up,
           ffn_w_down, cm_w_in, cm_conv_w, cm_conv_b, cm_conv_ln_g, cm_conv_ln_b, cm_sgu_ln_g, cm_sgu_ln_b,
           cm_sgu_w, cm_sgu_b, cm_w_out, attn_w_qkv, attn_w_o, attn_sink):
    n_prompt, prompt_len, _ = x_prompt.shape
    n_dec, dec_len, _ = x_sample.shape
    lay = _Layout(n_prompt, prompt_len, n_dec, dec_len)
    n_attn = attn_w_qkv.shape[0]
    n_cm = cm_w_in.shape[0]
    past = cache_k.shape[2]

    x = (x_prompt.reshape(lay.tp, D_MODEL), x_sample.reshape(lay.ts, D_MODEL))
    cond = jnp.concatenate([c_ctx[None, :], c, jnp.zeros((MOD_ROWS - 1 - n_dec, D_MODEL), F32)], axis=0)
    mod = _modulation(cond, w_mod, b_mod).reshape(DEPTH, MOD_ROWS, 1, N_MOD * D_MODEL)
    norm_rows = norm_w.reshape(DEPTH * 6, 1, D_MODEL)

    wg = ffn_w_gate.astype(BF16)
    wu = ffn_w_up.astype(BF16)
    wd = ffn_w_down.astype(BF16)
    w_in = cm_w_in.astype(BF16)
    w_out = cm_w_out.astype(BF16)
    sgu_w = cm_sgu_w.astype(BF16)
    sgu_bt = jnp.swapaxes(cm_sgu_b, 1, 2)
    w_qkv = attn_w_qkv.astype(BF16)
    w_o = attn_w_o.astype(BF16)
    vec_cm = lambda a: a.reshape(n_cm, 1, -1)
    conv_w = jnp.broadcast_to(cm_conv_w[:, :, None, :], (n_cm, CONV_WIDTH, SUBLANES, CONV_CH))
    ck = cache_k.reshape(n_dec, n_attn, past, KV_DIM).astype(BF16)
    cv = cache_v.reshape(n_dec, n_attn, past, KV_DIM).astype(BF16)
    qkv_tm = 1024
    cos_t, sin_t = _rope_tables(dec_len, qkv_tm)

    new_k, new_v = [], []
    for l in range(DEPTH):
        mod_l = mod[l]
        j = l // 2
        x = _ffn(lay, x, mod_l, norm_rows, wg, wu, wd, l, 0, 0)
        if l % 2 == 0:
            g, b_out = _cm_in(lay, x, mod_l, norm_rows, w_in, vec_cm(cm_sgu_ln_g), vec_cm(cm_sgu_ln_b),
                              sgu_w, sgu_bt, l, j)
            x = _cm_out(lay, x, mod_l, norm_rows, g, b_out, conv_w, vec_cm(cm_conv_b),
                        vec_cm(cm_conv_ln_g), vec_cm(cm_conv_ln_b), w_out, l, j)
        else:
            q, k, v, kb, vb = _qkv(lay, x, mod_l, norm_rows, w_qkv, cos_t, sin_t, l, j, qkv_tm)
            new_k.append(k.reshape(n_prompt, prompt_len, N_KV_HEADS, HEAD_DIM))
            new_v.append(v.reshape(n_prompt, prompt_len, N_KV_HEADS, HEAD_DIM))
            x = _attention(lay, x, mod_l, norm_rows, q, kb, vb, ck, cv, attn_sink, w_o, l, j)
        x = _ffn(lay, x, mod_l, norm_rows, wg, wu, wd, l, 1, 2, split_out=(l == DEPTH - 1))
    y_prompt, y_sample = x
    return (y_prompt.reshape(x_prompt.shape), y_sample.reshape(x_sample.shape),
            jnp.stack(new_k, axis=1), jnp.stack(new_v, axis=1))
```
